```python
import math
import jax, jax.numpy as jnp
from jax import lax
import numpy as np

D_MODEL = 1024
BATCH = 2
SEQ = 8192
DEPTH = 4
DEC_BATCH = 2
DEC_SEQ = 16384
PAST_LEN = 128

RET_HEADS = 4
RET_DK = 256
RET_DV = 256
RET_CHUNK = 128
RWKV_HEAD = 64
RWKV_HEADS = D_MODEL // RWKV_HEAD
RWKV_W = RWKV_HEADS * RWKV_HEAD
RWKV_LORA_W = 64
RWKV_LORA_A = 64
RWKV_LORA_G = 128
MLA_HEADS = 8
MLA_NOPE = 128
MLA_ROPE = 64
MLA_V = 128
MLA_Q_RANK = 512
MLA_KV_RANK = 256
ATTN_BLOCK = 128
D_FF = 4 * D_MODEL
ROPE_THETA = 10000.0
ALPHA = (2 * DEPTH) ** 0.25
BETA = (8 * DEPTH) ** -0.25
EPS = 1e-5

RET_QK_W = RET_HEADS * RET_DK
RET_V_W = RET_HEADS * RET_DV
RET_COLS = 2 * RET_QK_W + 2 * RET_V_W
RWKV_SHIFT_COLS = 3 * RWKV_W + RWKV_LORA_W + RWKV_LORA_A
RWKV_COLS = RWKV_SHIFT_COLS + RWKV_LORA_G
MLA_COLS = MLA_Q_RANK + MLA_KV_RANK + MLA_ROPE
IN_COLS = RET_COLS + RWKV_COLS + MLA_COLS

kernel_name = "hybrid_bidir_retention_rwkv7_mla_encoder"

f32 = jnp.float32


def layer_norm(x, g, b):
    xf = x.astype(f32)
    mu = jnp.mean(xf, -1, keepdims=True)
    var = jnp.mean(jnp.square(xf - mu), -1, keepdims=True)
    return ((xf - mu) * lax.rsqrt(var + EPS) * g.astype(f32) + b.astype(f32)).astype(x.dtype)


def rms_norm(x, g):
    xf = x.astype(f32)
    return (xf * lax.rsqrt(jnp.mean(xf * xf, -1, keepdims=True) + EPS) * g.astype(f32)).astype(x.dtype)


def head_norm(x, n_heads, g):
    shp = x.shape
    xf = x.astype(f32).reshape(*shp[:-1], n_heads, shp[-1] // n_heads)
    mu = jnp.mean(xf, -1, keepdims=True)
    var = jnp.mean(jnp.square(xf - mu), -1, keepdims=True)
    return ((xf - mu) * lax.rsqrt(var + EPS)).reshape(shp) * g.astype(f32)


def rope(x, pos):
    half = x.shape[-1] // 2
    inv = ROPE_THETA ** (-jnp.arange(half, dtype=f32) / half)
    ang = pos[:, None] * inv[None, :]
    cos = jnp.cos(ang)[:, None, :]
    sin = jnp.sin(ang)[:, None, :]
    xf = x.astype(f32)
    x1, x2 = xf[..., :half], xf[..., half:]
    return jnp.concatenate([x1 * cos - x2 * sin, x1 * sin + x2 * cos], -1).astype(x.dtype)


def retention_scan(q, k, v, log_gamma, include_diag):
    B, T, H, DK = q.shape
    DV = v.shape[-1]
    C = RET_CHUNK
    N = T // C
    idx = jnp.arange(C, dtype=f32)
    diff = idx[:, None] - idx[None, :]
    mask = (diff >= 0) if include_diag else (diff > 0)
    decay_in = jnp.where(mask[None], jnp.exp(log_gamma[:, None, None] * jnp.maximum(diff, 0.0)[None]), 0.0)
    q_scale = jnp.exp(log_gamma[None, :] * (idx + 1.0)[:, None])
    k_scale = jnp.exp(log_gamma[None, :] * (C - 1.0 - idx)[:, None])
    chunk_decay = jnp.exp(log_gamma * C)
    to_chunks = lambda t: t.reshape(B, N, C, H, t.shape[-1]).transpose(1, 0, 2, 3, 4)

    def step(R, xs):
        qb, kb, vb = xs
        s = jnp.einsum('bnhd,bmhd->bhnm', qb, kb) * decay_in
        inner = jnp.einsum('bhnm,bmhv->bnhv', s, vb)
        cross = jnp.einsum('bnhd,bhdv->bnhv', qb, R) * q_scale[None, :, :, None]
        R = R * chunk_decay[None, :, None, None] + jnp.einsum('bmhd,bmhv->bhdv', kb * k_scale[None, :, :, None], vb)
        return R, inner + cross

    R0 = jnp.zeros((B, H, DK, DV), f32)
    _, out = lax.scan(step, R0, (to_chunks(q), to_chunks(k), to_chunks(v)))
    return out.transpose(1, 0, 2, 3, 4).reshape(B, T, H, DV)


def retention_branch(u, pos, gn_g, w_o):
    B, T, _ = u.shape
    q, k, v, g = jnp.split(u, [RET_QK_W, 2 * RET_QK_W, 2 * RET_QK_W + RET_V_W], axis=-1)
    q = rope(q.reshape(B, T, RET_HEADS, RET_DK), pos).astype(f32)
    k = rope(k.reshape(B, T, RET_HEADS, RET_DK), pos).astype(f32) * (RET_DK ** -0.5)
    v = v.reshape(B, T, RET_HEADS, RET_DV).astype(f32)
    log_g = jnp.log1p(-jnp.exp2(-5.0 - jnp.arange(RET_HEADS, dtype=f32)))
    fwd = retention_scan(q, k, v, log_g, True)
    bwd = jnp.flip(retention_scan(jnp.flip(q, 1), jnp.flip(k, 1), jnp.flip(v, 1), log_g[::-1], False), 1)
    o = head_norm((fwd + bwd).reshape(B, T, RET_V_W), RET_HEADS, gn_g).astype(u.dtype)
    return (o * jax.nn.silu(g)) @ w_o


def token_shift(x, forward):
    zeros = jnp.zeros_like(x[:, :1])
    if forward:
        return jnp.concatenate([zeros, x[:, :-1]], axis=1)
    return jnp.concatenate([x[:, 1:], zeros], axis=1)


def rwkv7_scan(r, w, k, v, kk, a, reverse):
    B, T, H, N = r.shape
    xs = tuple(t.transpose(1, 0, 2, 3) for t in (r, w, k, v, kk, a))

    def step(S, inp):
        rt, wt, kt, vt, kkt, at = inp
        sa = jnp.einsum('bhvk,bhk->bhv', S, -kkt)
        S = S * wt[:, :, None, :] + sa[..., None] * (kkt * at)[:, :, None, :] + vt[..., None] * kt[:, :, None, :]
        return S, jnp.einsum('bhvk,bhk->bhv', S, rt)

    _, y = lax.scan(step, jnp.zeros((B, H, N, N), f32), xs, reverse=reverse)
    return y.transpose(1, 0, 2, 3)


def rwkv7_direction(us, forward, mu, w0, w_up, a0, a_up, k_k, k_a, r_k):
    B, T, _ = us.shape
    xd = us + (token_shift(us, forward) - us) * mu
    r, k, v, dw, da = jnp.split(xd, [RWKV_W, 2 * RWKV_W, 3 * RWKV_W, 3 * RWKV_W + RWKV_LORA_W], axis=-1)
    decay = jnp.exp(-math.exp(-0.5) * jax.nn.sigmoid((w0 + jnp.tanh(dw) @ w_up).astype(f32)))
    a = jax.nn.sigmoid((a0 + da @ a_up).astype(f32))
    heads = lambda t: t.astype(f32).reshape(B, T, RWKV_HEADS, RWKV_HEAD)
    r, k, v, decay, a = heads(r), heads(k), heads(v), heads(decay), heads(a)
    kkg = k_k.astype(f32).reshape(RWKV_HEADS, RWKV_HEAD)
    kag = k_a.astype(f32).reshape(RWKV_HEADS, RWKV_HEAD)
    rkg = r_k.astype(f32).reshape(RWKV_HEADS, RWKV_HEAD)
    kk = k * kkg
    kk = kk * lax.rsqrt(jnp.sum(kk * kk, -1, keepdims=True) + 1e-12)
    k_eff = k * (1.0 + (a - 1.0) * kag)
    y = rwkv7_scan(r, decay, k_eff, v, kk, a, reverse=not forward)
    bonus = jnp.sum(r * k_eff * rkg, -1, keepdims=True) * v
    return y, bonus


def rwkv7_branch(u, mu, w0, w_up, a0, a_up, g_up, k_k, k_a, r_k, lnx_g, lnx_b, w_o):
    B, T, _ = u.shape
    us, dg = jnp.split(u, [RWKV_SHIFT_COLS], axis=-1)
    g = jax.nn.sigmoid(dg) @ g_up
    y_f, b_f = rwkv7_direction(us, True, mu[0], w0[0], w_up[0], a0[0], a_up[0], k_k, k_a, r_k)
    y_b, b_b = rwkv7_direction(us, False, mu[1], w0[1], w_up[1], a0[1], a_up[1], k_k, k_a, r_k)
    o = head_norm((y_f + y_b).reshape(B, T, RWKV_W), RWKV_HEADS, lnx_g) + lnx_b.astype(f32) \
        + (b_f + b_b).reshape(B, T, RWKV_W)
    return (o.astype(u.dtype) * g) @ w_o


def dense_attention_blocks(q, k, v):
    B, T, H, Dq = q.shape
    nb = T // ATTN_BLOCK
    qb = q.reshape(B, nb, ATTN_BLOCK, H, Dq).transpose(1, 0, 2, 3, 4)
    scale = Dq ** -0.5

    def one_block(qi):
        s = jnp.einsum('bqhd,bkhd->bhqk', qi, k, preferred_element_type=f32) * scale
        p = jax.nn.softmax(s, axis=-1)
        return jnp.einsum('bhqk,bkhv->bqhv', p.astype(v.dtype), v)

    o = lax.map(one_block, qb)
    return o.transpose(1, 0, 2, 3, 4).reshape(B, T, H, v.shape[-1])


def mla_branch(u, pos, q_norm_g, w_uq, kv_norm_g, w_ukv, w_o):
    B, T, _ = u.shape
    qc, kvc, kr = jnp.split(u, [MLA_Q_RANK, MLA_Q_RANK + MLA_KV_RANK], axis=-1)
    q = (rms_norm(qc, q_norm_g) @ w_uq).reshape(B, T, MLA_HEADS, MLA_NOPE + MLA_ROPE)
    q = jnp.concatenate([q[..., :MLA_NOPE], rope(q[..., MLA_NOPE:], pos)], -1)
    kv = (rms_norm(kvc, kv_norm_g) @ w_ukv).reshape(B, T, MLA_HEADS, MLA_NOPE + MLA_V)
    k_nope, v = jnp.split(kv, [MLA_NOPE], axis=-1)
    k_rope = rope(kr[:, :, None, :], pos)
    k = jnp.concatenate([k_nope, jnp.broadcast_to(k_rope, (B, T, MLA_HEADS, MLA_ROPE))], -1)
    o = dense_attention_blocks(q, k, v)
    return o.reshape(B, T, MLA_HEADS * MLA_V) @ w_o


def token_mixer(h, pos, w_in, ret_gn, ret_wo, rwkv_mu, rwkv_w0, rwkv_wup, rwkv_a0, rwkv_aup,
                rwkv_gup, rwkv_kk, rwkv_ka, rwkv_rk, rwkv_lnx_g, rwkv_lnx_b, rwkv_wo,
                mla_qnorm, mla_wuq, mla_kvnorm, mla_wukv, mla_wo, w_merge, b_merge, w_out):
    u = h @ w_in
    u_ret, u_rwkv, u_mla = jnp.split(u, [RET_COLS, RET_COLS + RWKV_COLS], axis=-1)
    y_a = retention_branch(u_ret, pos, ret_gn, ret_wo)
    y_b = rwkv7_branch(u_rwkv, rwkv_mu, rwkv_w0, rwkv_wup, rwkv_a0, rwkv_aup, rwkv_gup,
                       rwkv_kk, rwkv_ka, rwkv_rk, rwkv_lnx_g, rwkv_lnx_b, rwkv_wo)
    y_c = mla_branch(u_mla, pos, mla_qnorm, mla_wuq, mla_kvnorm, mla_wukv, mla_wo)
    g_a, g_b, g_c = jnp.split(jax.nn.sigmoid(h @ w_merge + b_merge), 3, axis=-1)
    return (g_a * y_a + g_b * y_b + g_c * y_c) @ w_out


def sq_relu_mlp(h, w1, w2):
    return jnp.square(jax.nn.relu(h @ w1)) @ w2


def encoder_trunk(x, c, w_ada, b_ada, w_in, ret_gn, ret_wo, rwkv_mu, rwkv_w0, rwkv_wup, rwkv_a0,
                  rwkv_aup, rwkv_gup, rwkv_kk, rwkv_ka, rwkv_rk, rwkv_lnx_g, rwkv_lnx_b, rwkv_wo,
                  mla_qnorm, mla_wuq, mla_kvnorm, mla_wukv, mla_wo, w_merge, b_merge, w_out,
                  ln_g, ln_b, w_mlp1, w_mlp2):
    T = x.shape[1]
    pos = jnp.arange(T, dtype=f32)
    for l in range(DEPTH):
        ada = jax.nn.silu(c) @ w_ada[l] + b_ada[l]
        sh1, sc1, g1, sh2, sc2, g2 = jnp.split(ada[:, None, :], 6, axis=-1)
        h = x * (1.0 + sc1) + sh1
        mix = token_mixer(h, pos, w_in[l], ret_gn[l], ret_wo[l], rwkv_mu[l], rwkv_w0[l], rwkv_wup[l],
                          rwkv_a0[l], rwkv_aup[l], rwkv_gup[l], rwkv_kk[l], rwkv_ka[l], rwkv_rk[l],
                          rwkv_lnx_g[l], rwkv_lnx_b[l], rwkv_wo[l], mla_qnorm[l], mla_wuq[l],
                          mla_kvnorm[l], mla_wukv[l], mla_wo[l], w_merge[l], b_merge[l], w_out[l])
        x = layer_norm(ALPHA * x + (1.0 + g1) * mix, ln_g[l, 0], ln_b[l, 0])
        h = x * (1.0 + sc2) + sh2
        x = layer_norm(ALPHA * x + (1.0 + g2) * sq_relu_mlp(h, w_mlp1[l], w_mlp2[l]), ln_g[l, 1], ln_b[l, 1])
    return x


def setup_inputs(seed: int = 0) -> dict:
    key = jax.random.key(seed)
    ks = iter(jax.random.split(key, 64))
    L, D = DEPTH, D_MODEL

    def nrm(shape, scale):
        return jax.random.normal(next(ks), shape, f32) * scale

    def gain(shape):
        return 1.0 + nrm(shape, 0.05)

    return {
        "x_prompt": nrm((BATCH, SEQ, D), 1.0),
        "x_sample": nrm((DEC_BATCH, DEC_SEQ, D), 1.0),
        "c_prompt": nrm((BATCH, D), 1.0),
        "c_sample": nrm((DEC_BATCH, D), 1.0),
        "w_ada": nrm((L, D, 6 * D), 0.5 * D ** -0.5),
        "b_ada": nrm((L, 6 * D), 0.01),
        "w_in": nrm((L, D, IN_COLS), D ** -0.5),
        "ret_gn": gain((L, RET_V_W)),
        "ret_wo": nrm((L, RET_V_W, D), BETA * RET_V_W ** -0.5),
        "rwkv_mu": jax.random.uniform(next(ks), (L, 2, RWKV_SHIFT_COLS), f32),
        "rwkv_w0": nrm((L, 2, RWKV_W), 0.5),
        "rwkv_wup": nrm((L, 2, RWKV_LORA_W, RWKV_W), RWKV_LORA_W ** -0.5),
        "rwkv_a0": nrm((L, 2, RWKV_W), 0.1),
        "rwkv_aup": nrm((L, 2, RWKV_LORA_A, RWKV_W), RWKV_LORA_A ** -0.5),
        "rwkv_gup": nrm((L, RWKV_LORA_G, RWKV_W), RWKV_LORA_G ** -0.5),
        "rwkv_kk": gain((L, RWKV_W)),
        "rwkv_ka": gain((L, RWKV_W)),
        "rwkv_rk": nrm((L, RWKV_W), 0.1),
        "rwkv_lnx_g": gain((L, RWKV_W)),
        "rwkv_lnx_b": nrm((L, RWKV_W), 0.01),
        "rwkv_wo": nrm((L, RWKV_W, D), BETA * RWKV_W ** -0.5),
        "mla_qnorm": gain((L, MLA_Q_RANK)),
        "mla_wuq": nrm((L, MLA_Q_RANK, MLA_HEADS * (MLA_NOPE + MLA_ROPE)), MLA_Q_RANK ** -0.5),
        "mla_kvnorm": gain((L, MLA_KV_RANK)),
        "mla_wukv": nrm((L, MLA_KV_RANK, MLA_HEADS * (MLA_NOPE + MLA_V)), MLA_KV_RANK ** -0.5),
        "mla_wo": nrm((L, MLA_HEADS * MLA_V, D), BETA * (MLA_HEADS * MLA_V) ** -0.5),
        "w_merge": nrm((L, D, 3 * D), D ** -0.5),
        "b_merge": nrm((L, 3 * D), 0.01),
        "w_out": nrm((L, D, D), BETA * D ** -0.5),
        "ln_g": gain((L, 2, D)),
        "ln_b": nrm((L, 2, D), 0.01),
        "w_mlp1": nrm((L, D, D_FF), D ** -0.5),
        "w_mlp2": nrm((L, D_FF, D), BETA * D_FF ** -0.5),
    }


def reference(x_prompt, x_sample, c_prompt, c_sample, w_ada, b_ada, w_in, ret_gn, ret_wo, rwkv_mu,
              rwkv_w0, rwkv_wup, rwkv_a0, rwkv_aup, rwkv_gup, rwkv_kk, rwkv_ka, rwkv_rk, rwkv_lnx_g,
              rwkv_lnx_b, rwkv_wo, mla_qnorm, mla_wuq, mla_kvnorm, mla_wukv, mla_wo, w_merge, b_merge,
              w_out, ln_g, ln_b, w_mlp1, w_mlp2):
    weights = (w_ada, b_ada, w_in, ret_gn, ret_wo, rwkv_mu, rwkv_w0, rwkv_wup, rwkv_a0, rwkv_aup,
               rwkv_gup, rwkv_kk, rwkv_ka, rwkv_rk, rwkv_lnx_g, rwkv_lnx_b, rwkv_wo, mla_qnorm,
               mla_wuq, mla_kvnorm, mla_wukv, mla_wo, w_merge, b_merge, w_out, ln_g, ln_b,
               w_mlp1, w_mlp2)
    y_prompt = encoder_trunk(x_prompt, c_prompt, *weights)
    y_sample = encoder_trunk(x_sample, c_sample, *weights)
    return (y_prompt, y_sample)
```

```python
import functools
import math

import jax
import jax.numpy as jnp
from jax import lax
from jax.experimental import pallas as pl
from jax.experimental.pallas import tpu as pltpu

F32 = jnp.float32
BF16 = jnp.bfloat16
HI = lax.Precision.HIGHEST

D_MODEL = 1024
DEPTH = 4
RET_HEADS = 4
RET_DK = 256
RET_CHUNK = 128
RWKV_HEAD = 64
RWKV_HEADS = 16
RWKV_W = 1024
RWKV_PAIRS = RWKV_HEADS // 2
RWKV_CHUNK = 64
MLA_HEADS = 8
MLA_NOPE = 128
MLA_ROPE = 64
MLA_V = 128
MLA_Q_RANK = 512
MLA_KV_RANK = 256
MLA_SLOT = 256
D_FF = 4 * D_MODEL
ROPE_THETA = 10000.0
ALPHA = (2 * DEPTH) ** 0.25
EPS = 1e-5
LANES = 128

U_RET_G = 3
U_RWKV_R = 4
U_QC = 7168
U_KV = 7680
U_LORA = 8064
U_DG = 8192
U_COLS = 8448

VMEM_LIMIT = 56 * 1024 * 1024


def _cparams(sem):
    return pltpu.CompilerParams(dimension_semantics=sem, vmem_limit_bytes=VMEM_LIMIT)


def _dot(a, b, precision=None):
    return jnp.dot(a, b, preferred_element_type=F32, precision=precision)


def _dot_nt(a, b, precision=None):
    return lax.dot_general(a, b, (((1,), (1,)), ((), ())), preferred_element_type=F32, precision=precision)


def _dot_tn(a, b, precision=None):
    return lax.dot_general(a, b, (((0,), (0,)), ((), ())), preferred_element_type=F32, precision=precision)


def _sigmoid(x):
    return 1.0 / (1.0 + jnp.exp(-x))


def _layer_norm(x, g, b):
    mu = jnp.mean(x, -1, keepdims=True)
    xc = x - mu
    var = jnp.mean(xc * xc, -1, keepdims=True)
    return xc * lax.rsqrt(var + EPS) * g + b


def _ada_kernel(c_ref, w_ref, b_ref, o_ref):
    c = c_ref[...]
    o_ref[0] = _dot(c * _sigmoid(c), w_ref[0], HI) + b_ref[0]


def _ada_all(c_pad, w_ada, b_ada):
    L, D, N = w_ada.shape
    tn = 1536
    return pl.pallas_call(
        _ada_kernel,
        grid=(L, N // tn),
        in_specs=[
            pl.BlockSpec((8, D), lambda l, j: (0, 0)),
            pl.BlockSpec((1, D, tn), lambda l, j: (l, 0, j)),
            pl.BlockSpec((1, 1, tn), lambda l, j: (l, 0, j)),
        ],
        out_specs=pl.BlockSpec((1, 8, tn), lambda l, j: (l, 0, j)),
        out_shape=jax.ShapeDtypeStruct((L, 8, N), F32),
        compiler_params=_cparams(("parallel", "parallel")),
        name="ada",
    )(c_pad, w_ada, b_ada.reshape(L, 1, N))


def _inproj_kernel(x_ref, sc_ref, sh_ref, w_ref, b_ref, o_ref, h_ref, *, gate):
    @pl.when(pl.program_id(2) == 0)
    def _():
        h_ref[...] = (x_ref[0] * (1.0 + sc_ref[0]) + sh_ref[0]).astype(BF16)

    acc = _dot(h_ref[...], w_ref[...])
    if gate:
        acc = _sigmoid(acc + b_ref[...])
    o_ref[0] = acc.astype(o_ref.dtype)


def _inproj(x, sc, sh, w, b, *, gate, tn, name):
    B, T, D = x.shape
    N = w.shape[1]
    tm = min(T, 1024)
    return pl.pallas_call(
        functools.partial(_inproj_kernel, gate=gate),
        grid=(B, T // tm, N // tn),
        in_specs=[
            pl.BlockSpec((1, tm, D), lambda b_, i, j: (b_, i, 0)),
            pl.BlockSpec((1, 1, D), lambda b_, i, j: (b_, 0, 0)),
            pl.BlockSpec((1, 1, D), lambda b_, i, j: (b_, 0, 0)),
            pl.BlockSpec((D, tn), lambda b_, i, j: (0, j)),
            pl.BlockSpec((1, tn), lambda b_, i, j: (0, j)),
        ],
        out_specs=pl.BlockSpec((1, tm, tn), lambda b_, i, j: (b_, i, j)),
        out_shape=jax.ShapeDtypeStruct((B, T, N), F32),
        scratch_shapes=[pltpu.VMEM((tm, D), BF16)],
        compiler_params=_cparams(("parallel", "parallel", "arbitrary")),
        name=name,
    )(x, sc, sh, w, b)


def _ret_kernel(lg_ref, q_ref, k_ref, v_ref, cos_ref, sin_ref, o_ref, state_ref, *, rev, tb):
    C = RET_CHUNK
    half = RET_DK // 2

    @pl.when(pl.program_id(2) == 0)
    def _():
        state_ref[...] = jnp.zeros_like(state_ref)

    lg = lg_ref[pl.program_id(1)]
    row = lax.broadcasted_iota(jnp.int32, (C, C), 0).astype(F32)
    col = lax.broadcasted_iota(jnp.int32, (C, C), 1).astype(F32)
    idx = lax.broadcasted_iota(jnp.int32, (C, 1), 0).astype(F32)
    if rev:
        dist = col - row
        decay_in = jnp.where(dist > 0, jnp.exp(lg * jnp.maximum(dist, 0.0)), 0.0)
        q_scale = jnp.exp(lg * (C - idx))
        k_scale = jnp.exp(lg * idx)
    else:
        dist = row - col
        decay_in = jnp.where(dist >= 0, jnp.exp(lg * jnp.maximum(dist, 0.0)), 0.0)
        q_scale = jnp.exp(lg * (idx + 1.0))
        k_scale = jnp.exp(lg * (C - 1.0 - idx))
    chunk_decay = jnp.exp(lg * C)

    def rot(x, cos, sin):
        x1, x2 = x[:, :half], x[:, half:]
        return jnp.concatenate([x1 * cos - x2 * sin, x1 * sin + x2 * cos], -1)

    n_chunks = tb // C
    for cc in range(n_chunks):
        c = n_chunks - 1 - cc if rev else cc
        rows = slice(c * C, (c + 1) * C)
        cos = cos_ref[rows, :]
        sin = sin_ref[rows, :]
        q = rot(q_ref[0, rows, :], cos, sin)
        k = rot(k_ref[0, rows, :], cos, sin) * (RET_DK ** -0.5)
        v = v_ref[0, rows, :]
        s = _dot_nt(q, k, HI) * decay_in
        inner = _dot(s, v, HI)
        state = state_ref[...]
        cross = _dot(q, state, HI) * q_scale
        state_ref[...] = state * chunk_decay + _dot_tn(k * k_scale, v, HI)
        o_ref[0, rows, :] = inner + cross


def _retention(u, cos, sin, lg, *, rev):
    B, T, _ = u.shape
    tb = min(T, 512)
    nt = T // tb
    tmap = (lambda i: nt - 1 - i) if rev else (lambda i: i)
    return pl.pallas_call(
        functools.partial(_ret_kernel, rev=rev, tb=tb),
        grid=(B, RET_HEADS, nt),
        in_specs=[
            pl.BlockSpec(memory_space=pltpu.SMEM),
            pl.BlockSpec((1, tb, RET_DK), lambda b_, h, i: (b_, tmap(i), h)),
            pl.BlockSpec((1, tb, RET_DK), lambda b_, h, i: (b_, tmap(i), RET_HEADS + h)),
            pl.BlockSpec((1, tb, RET_DK), lambda b_, h, i: (b_, tmap(i), 2 * RET_HEADS + h)),
            pl.BlockSpec((tb, RET_DK // 2), lambda b_, h, i: (tmap(i), 0)),
            pl.BlockSpec((tb, RET_DK // 2), lambda b_, h, i: (tmap(i), 0)),
        ],
        out_specs=pl.BlockSpec((1, tb, RET_DK), lambda b_, h, i: (b_, tmap(i), h)),
        out_shape=jax.ShapeDtypeStruct((B, T, RET_HEADS * RET_DK), F32),
        scratch_shapes=[pltpu.VMEM((RET_DK, RET_DK), F32)],
        compiler_params=_cparams(("parallel", "parallel", "arbitrary")),
        name="ret_bwd" if rev else "ret_fwd",
    )(lg, u, u, u, cos, sin)


def _rwkv_prep_kernel(r_ref, k_ref, v_ref, l_ref, dg_ref, prev_ref, next_ref, mu_ref, w0_ref, a0_ref,
                      wup_ref, aup_ref, gup_ref, kkg_ref, kag_ref, *out_refs, tm):
    g_ref = out_refs[-1]
    W = RWKV_W
    xs = (r_ref[0], k_ref[0], v_ref[0], l_ref[0])
    offs = (0, W, 2 * W, 3 * W)
    row = lax.broadcasted_iota(jnp.int32, (tm, 1), 0)
    lane = lax.broadcasted_iota(jnp.int32, (tm, LANES), 1)
    for d in range(2):
        if d == 0:
            nb, edge, shift = prev_ref[0, 0], 0, 1
        else:
            nb, edge, shift = next_ref[0, 0], tm - 1, tm - 1
        mu = mu_ref[d]
        xd = []
        for x, off in zip(xs, offs):
            w = x.shape[1]
            shifted = jnp.where(row == edge, nb[:, off:off + w], pltpu.roll(x, shift, 0))
            xd.append(x + (shifted - x) * mu[:, off:off + w])
        xr, xk, xv, xl = xd
        lhs = jnp.where(lane < 64, jnp.tanh(xl), xl)
        lw = -math.exp(-0.5) * _sigmoid(w0_ref[d] + _dot(lhs, wup_ref[d], HI))
        a = _sigmoid(a0_ref[d] + _dot(lhs, aup_ref[d], HI))
        o = out_refs[6 * d:6 * d + 6]
        o[0][0] = xr
        o[1][0] = lw
        o[2][0] = xk * (1.0 + (a - 1.0) * kag_ref[...])
        o[3][0] = xv
        o[4][0] = xk * kkg_ref[...]
        o[5][0] = a
    g_ref[0] = _dot(_sigmoid(dg_ref[0]), gup_ref[...], HI)


def _rwkv_prep(u, prev_rows, next_rows, mu, w0, a0, wup, aup, gup, kkg, kag, *, tm):
    B, T, _ = u.shape
    W = RWKV_W
    full = lambda shape: pl.BlockSpec(shape, lambda b_, i: (0,) * len(shape))
    ublk = lambda width, cb: pl.BlockSpec((1, tm, width), lambda b_, i: (b_, i, cb))
    edge = pl.BlockSpec((1, 1, 1, 3 * W + LANES), lambda b_, i: (b_, i, 0, 0))
    out = jax.ShapeDtypeStruct((B, T, W), F32)
    return pl.pallas_call(
        functools.partial(_rwkv_prep_kernel, tm=tm),
        grid=(B, T // tm),
        in_specs=[
            ublk(W, U_RWKV_R), ublk(W, U_RWKV_R + 1), ublk(W, U_RWKV_R + 2),
            ublk(LANES, U_LORA // LANES), ublk(LANES, U_DG // LANES),
            edge, edge,
            full((2, 1, 3 * W + LANES)), full((2, 1, W)), full((2, 1, W)),
            full((2, LANES, W)), full((2, LANES, W)), full((LANES, W)),
            full((1, W)), full((1, W)),
        ],
        out_specs=[pl.BlockSpec((1, tm, W), lambda b_, i: (b_, i, 0))] * 13,
        out_shape=[out] * 13,
        compiler_params=_cparams(("parallel", "parallel")),
        name="rwkv_prep",
    )(u, u, u, u, u, prev_rows, next_rows, mu, w0, a0, wup, aup, gup, kkg, kag)


def _rwkv_kernel(r_ref, lw_ref, ke_ref, v_ref, kk_ref, a_ref, rk_ref, y_ref, bon_ref, state_ref, *, rev, tb):
    C = RWKV_CHUNK
    N = 2 * C

    @pl.when(pl.program_id(2) == 0)
    def _():
        state_ref[...] = jnp.zeros_like(state_ref)

    rowi = lax.broadcasted_iota(jnp.int32, (N, N), 0)
    coli = lax.broadcasted_iota(jnp.int32, (N, N), 1)
    same_head = (rowi // C) == (coli // C)
    tr, tc = rowi % C, coli % C
    before = (tc > tr) if rev else (tc < tr)
    strict = (same_head & before).astype(F32)
    incl = (same_head & (before | (tc == tr))).astype(F32)
    eye = (rowi == coli).astype(F32)
    block_ones = same_head.astype(F32)
    level_masks = []
    b = 1
    while b < C:
        level_masks.append(((rowi // (2 * b) == coli // (2 * b)) & (rowi // b != coli // b)).astype(F32))
        b *= 2
    ci = lax.broadcasted_iota(jnp.int32, (C, C), 0)
    cj = lax.broadcasted_iota(jnp.int32, (C, C), 1)
    cum_mat = ((cj >= ci) if rev else (cj <= ci)).astype(F32)
    lane = lax.broadcasted_iota(jnp.int32, (C, N), 1)
    m0 = (lane < C).astype(F32)
    m1 = 1.0 - m0
    rk = rk_ref[...]

    def stack(x):
        return jnp.concatenate([x * m0, x * m1], axis=0)

    n_chunks = tb // C

    def body(cc, carry):
        c = n_chunks - 1 - cc if rev else cc
        rows = pl.ds(pl.multiple_of(c * C, C), C)
        r = r_ref[0, rows, :]
        lw = lw_ref[0, rows, :]
        ke = ke_ref[0, rows, :]
        v = v_ref[0, rows, :]
        kk = kk_ref[0, rows, :]
        a = a_ref[0, rows, :]

        kk = kk * lax.rsqrt(_dot(kk * kk, block_ones, HI) + 1e-12)
        bon_ref[0, rows, :] = _dot(r * ke * rk, block_ones, HI) * v
        alpha = -kk
        beta = kk * a

        cum = _dot(cum_mat, lw, HI)
        tot = jnp.sum(lw, axis=0, keepdims=True)
        e_neg = jnp.exp(-cum)
        e_end = jnp.exp(tot - cum)
        a_st = stack(alpha * jnp.exp(cum - lw))
        r_st = stack(r * jnp.exp(cum))
        b_st = stack(beta * e_neg)
        k_st = stack(ke * e_neg)
        v_st = stack(v)
        bh_st = stack(beta * e_end)
        kh_st = stack(ke * e_end)

        gram = _dot_nt(jnp.concatenate([a_st, r_st], 0), jnp.concatenate([b_st, k_st], 0), HI)
        m_ab = gram[:N, :N] * strict
        m_ak = gram[:N, N:] * strict
        n_rb = gram[N:, :N] * incl
        n_rk = gram[N:, N:] * incl

        x = eye + m_ab * level_masks[0]
        for mask in level_masks[1:]:
            x = x + _dot(_dot(x, m_ab * mask, HI), x, HI)

        w1 = _dot(x, a_st, HI)
        w2 = _dot(x, _dot(m_ak, v_st, HI), HI)
        s = state_ref[...]
        u = _dot_nt(w1, s, HI) + w2
        y = _dot_nt(r_st, s, HI) + _dot(n_rb, u, HI) + _dot(n_rk, v_st, HI)
        state_ref[...] = s * jnp.exp(tot) + _dot_tn(jnp.concatenate([u, v_st], 0),
                                                    jnp.concatenate([bh_st, kh_st], 0), HI)
        y_ref[0, rows, :] = y[:C] + y[C:]
        return carry

    lax.fori_loop(0, n_chunks, body, 0)


def _rwkv_scan(r, lw, ke, v, kk, a, rk, *, rev):
    B, T, W = r.shape
    tb = min(T, 512)
    nt = T // tb
    tmap = (lambda i: nt - 1 - i) if rev else (lambda i: i)
    blk = pl.BlockSpec((1, tb, LANES), lambda b_, p, i: (b_, tmap(i), p))
    out = jax.ShapeDtypeStruct((B, T, W), F32)
    return pl.pallas_call(
        functools.partial(_rwkv_kernel, rev=rev, tb=tb),
        grid=(B, RWKV_PAIRS, nt),
        in_specs=[blk] * 6 + [pl.BlockSpec((1, LANES), lambda b_, p, i: (0, p))],
        out_specs=[blk, blk],
        out_shape=[out, out],
        scratch_shapes=[pltpu.VMEM((LANES, LANES), F32)],
        compiler_params=_cparams(("parallel", "parallel", "arbitrary")),
        name="rwkv_bwd" if rev else "rwkv_fwd",
    )(r, lw, ke, v, kk, a, rk)


def _mla_prep_kernel(qc_ref, kv_ref, gq_ref, gkv_ref, wq_ref, wk_ref, wv_ref, ctq_ref, stq_ref, ctk_ref,
                     stk_ref, q_ref, k_ref, v_ref):
    def rms(x, g):
        return x * lax.rsqrt(jnp.mean(x * x, -1, keepdims=True) + EPS) * g

    qn = rms(qc_ref[0], gq_ref[...]).astype(BF16)
    kvx = kv_ref[0]
    kvn = rms(kvx[:, :MLA_KV_RANK], gkv_ref[...])
    kin = jnp.concatenate([kvn, kvx[:, MLA_KV_RANK:]], -1).astype(BF16)
    q = _dot(qn, wq_ref[...])
    k = _dot(kin, wk_ref[...])
    v_ref[0] = _dot(kvn.astype(BF16), wv_ref[...]).astype(BF16)
    ctq, stq, ctk, stk = ctq_ref[...], stq_ref[...], ctk_ref[...], stk_ref[...]
    for h in range(MLA_HEADS):
        sl = slice(h * MLA_SLOT, (h + 1) * MLA_SLOT)
        qh, kh = q[:, sl], k[:, sl]
        q_ref[0, :, sl] = (qh * ctq + pltpu.roll(qh, MLA_SLOT - MLA_ROPE, 1) * stq).astype(BF16)
        k_ref[0, :, sl] = (kh * ctk + pltpu.roll(kh, MLA_SLOT - MLA_ROPE, 1) * stk).astype(BF16)


def _mla_prep(u, gq, gkv, wq, wk, wv, ctq, stq, ctk, stk):
    B, T, _ = u.shape
    tm = min(T, 512)
    kvw = MLA_KV_RANK + 2 * MLA_ROPE
    QW = MLA_HEADS * MLA_SLOT
    VW = MLA_HEADS * MLA_V
    full = lambda shape: pl.BlockSpec(shape, lambda b_, i: (0,) * len(shape))
    tab = pl.BlockSpec((tm, MLA_SLOT), lambda b_, i: (i, 0))
    return pl.pallas_call(
        _mla_prep_kernel,
        grid=(B, T // tm),
        in_specs=[
            pl.BlockSpec((1, tm, MLA_Q_RANK), lambda b_, i: (b_, i, U_QC // MLA_Q_RANK)),
            pl.BlockSpec((1, tm, kvw), lambda b_, i: (b_, i, U_KV // kvw)),
            full((1, MLA_Q_RANK)), full((1, MLA_KV_RANK)),
            full((MLA_Q_RANK, QW)), full((kvw, QW)), full((MLA_KV_RANK, VW)),
            tab, tab, tab, tab,
        ],
        out_specs=[
            pl.BlockSpec((1, tm, QW), lambda b_, i: (b_, i, 0)),
            pl.BlockSpec((1, tm, QW), lambda b_, i: (b_, i, 0)),
            pl.BlockSpec((1, tm, VW), lambda b_, i: (b_, i, 0)),
        ],
        out_shape=[jax.ShapeDtypeStruct((B, T, QW), BF16), jax.ShapeDtypeStruct((B, T, QW), BF16),
                   jax.ShapeDtypeStruct((B, T, VW), BF16)],
        compiler_params=_cparams(("parallel", "parallel")),
        name="mla_prep",
    )(u, u, gq, gkv, wq, wk, wv, ctq, stq, ctk, stk)


def _flash_kernel(q_ref, k_ref, v_ref, o_ref, m_ref, l_ref, acc_ref):
    j = pl.program_id(3)

    @pl.when(j == 0)
    def _():
        m_ref[...] = jnp.full_like(m_ref, -jnp.inf)
        l_ref[...] = jnp.zeros_like(l_ref)
        acc_ref[...] = jnp.zeros_like(acc_ref)

    s = _dot_nt(q_ref[0], k_ref[0])
    m_prev = m_ref[...]
    m_new = jnp.maximum(m_prev, jnp.max(s, -1, keepdims=True))
    p = jnp.exp(s - m_new)
    corr = jnp.exp(m_prev - m_new)
    l_ref[...] = corr * l_ref[...] + jnp.sum(p, -1, keepdims=True)
    acc_ref[...] = corr * acc_ref[...] + _dot(p.astype(BF16), v_ref[0])
    m_ref[...] = m_new

    @pl.when(j == pl.num_programs(3) - 1)
    def _():
        o_ref[0] = (acc_ref[...] / l_ref[...]).astype(o_ref.dtype)


def _flash(q, k, v):
    B, T, _ = q.shape
    tq = min(T, 1024)
    tk = min(T, 512)
    return pl.pallas_call(
        _flash_kernel,
        grid=(B, MLA_HEADS, T // tq, T // tk),
        in_specs=[
            pl.BlockSpec((1, tq, MLA_SLOT), lambda b_, h, i, j: (b_, i, h)),
            pl.BlockSpec((1, tk, MLA_SLOT), lambda b_, h, i, j: (b_, j, h)),
            pl.BlockSpec((1, tk, MLA_V), lambda b_, h, i, j: (b_, j, h)),
        ],
        out_specs=pl.BlockSpec((1, tq, MLA_V), lambda b_, h, i, j: (b_, i, h)),
        out_shape=jax.ShapeDtypeStruct((B, T, MLA_HEADS * MLA_V), BF16),
        scratch_shapes=[pltpu.VMEM((tq, 1), F32), pltpu.VMEM((tq, 1), F32), pltpu.VMEM((tq, MLA_V), F32)],
        compiler_params=_cparams(("parallel", "parallel", "parallel", "arbitrary")),
        name="mla_attn",
    )(q, k, v)


def _seg_sum(x, ones_bd):
    hi = x.astype(BF16)
    lo = (x - hi.astype(F32)).astype(BF16)
    return _dot(hi, ones_bd) + _dot(lo, ones_bd)


def _mix_out_kernel(rf_ref, rb_ref, rg_ref, yf_ref, yb_ref, bf_ref, bb_ref, wg_ref, at_ref, gt_ref, x_ref,
                    g1_ref, gn_ref, lxg_ref, lxb_ref, lng_ref, lnb_ref, ones_ref,
                    wa_ref, wb_ref, wc_ref, wo_ref, o_ref):
    D = D_MODEL
    ret = rf_ref[0] + rb_ref[0]
    parts = []
    for h in range(RET_HEADS):
        xh = ret[:, h * RET_DK:(h + 1) * RET_DK]
        mu = jnp.mean(xh, -1, keepdims=True)
        xc = xh - mu
        var = jnp.mean(xc * xc, -1, keepdims=True)
        parts.append(xc * lax.rsqrt(var + EPS))
    rg = rg_ref[0]
    a_in = jnp.concatenate(parts, -1) * gn_ref[...] * (rg * _sigmoid(rg))
    y_a = _dot(a_in.astype(BF16), wa_ref[...])

    y = yf_ref[0] + yb_ref[0]
    ones_bd = ones_ref[...]
    mu = _seg_sum(y, ones_bd) * (1.0 / RWKV_HEAD)
    yc = y - mu
    var = _seg_sum(yc * yc, ones_bd) * (1.0 / RWKV_HEAD)
    o_b = yc * lax.rsqrt(var + EPS) * lxg_ref[...] + lxb_ref[...] + bf_ref[0] + bb_ref[0]
    y_b = _dot((o_b * wg_ref[0]).astype(BF16), wb_ref[...])

    y_c = _dot(at_ref[0], wc_ref[...])

    gt = gt_ref[0]
    merged = gt[:, :D] * y_a + gt[:, D:2 * D] * y_b + gt[:, 2 * D:] * y_c
    mix = _dot(merged.astype(BF16), wo_ref[...])
    o_ref[0] = _layer_norm(ALPHA * x_ref[0] + (1.0 + g1_ref[0]) * mix, lng_ref[...], lnb_ref[...])


def _mix_out(rf, rb, u, yf, yb, bf, bb, wg, attn, gates, x, g1, gn, lxg, lxb, lng, lnb, ones_bd, wa, wb, wc, wo):
    B, T, D = x.shape
    tm = min(T, 256)
    tok = lambda width, cb=0: pl.BlockSpec((1, tm, width), lambda b_, i: (b_, i, cb))
    vec = pl.BlockSpec((1, D), lambda b_, i: (0, 0))
    mat = pl.BlockSpec((D, D), lambda b_, i: (0, 0))
    return pl.pallas_call(
        _mix_out_kernel,
        grid=(B, T // tm),
        in_specs=[tok(D), tok(D), tok(D, U_RET_G), tok(D), tok(D), tok(D), tok(D), tok(D), tok(D), tok(3 * D),
                  tok(D), pl.BlockSpec((1, 1, D), lambda b_, i: (b_, 0, 0)),
                  vec, vec, vec, vec, vec, mat, mat, mat, mat, mat],
        out_specs=tok(D),
        out_shape=jax.ShapeDtypeStruct((B, T, D), F32),
        compiler_params=_cparams(("parallel", "parallel")),
        name="mix_out",
    )(rf, rb, u, yf, yb, bf, bb, wg, attn, gates, x, g1, gn, lxg, lxb, lng, lnb, ones_bd, wa, wb, wc, wo)


def _mlp_kernel(x_ref, sc_ref, sh_ref, g2_ref, w1_ref, w2_ref, lng_ref, lnb_ref, o_ref, h_ref, acc_ref):
    f = pl.program_id(2)

    @pl.when(f == 0)
    def _():
        h_ref[...] = (x_ref[0] * (1.0 + sc_ref[0]) + sh_ref[0]).astype(BF16)
        acc_ref[...] = jnp.zeros_like(acc_ref)

    a = jnp.maximum(_dot(h_ref[...], w1_ref[...]), 0.0)
    acc_ref[...] += _dot((a * a).astype(BF16), w2_ref[...])

    @pl.when(f == pl.num_programs(2) - 1)
    def _():
        o_ref[0] = _layer_norm(ALPHA * x_ref[0] + (1.0 + g2_ref[0]) * acc_ref[...], lng_ref[...], lnb_ref[...])


def _mlp(x, sc, sh, g2, w1, w2, lng, lnb):
    B, T, D = x.shape
    F = w1.shape[1]
    tm = min(T, 1024)
    tf = 1024
    ada = pl.BlockSpec((1, 1, D), lambda b_, i, f: (b_, 0, 0))
    vec = pl.BlockSpec((1, D), lambda b_, i, f: (0, 0))
    return pl.pallas_call(
        _mlp_kernel,
        grid=(B, T // tm, F // tf),
        in_specs=[
            pl.BlockSpec((1, tm, D), lambda b_, i, f: (b_, i, 0)), ada, ada, ada,
            pl.BlockSpec((D, tf), lambda b_, i, f: (0, f)),
            pl.BlockSpec((tf, D), lambda b_, i, f: (f, 0)),
            vec, vec,
        ],
        out_specs=pl.BlockSpec((1, tm, D), lambda b_, i, f: (b_, i, 0)),
        out_shape=jax.ShapeDtypeStruct((B, T, D), F32),
        scratch_shapes=[pltpu.VMEM((tm, D), BF16), pltpu.VMEM((tm, D), F32)],
        compiler_params=_cparams(("parallel", "parallel", "arbitrary")),
        name="mlp",
    )(x, sc, sh, g2, w1, w2, lng, lnb)


def _rot_half_cols(w):
    half = w.shape[-1] // 2
    return jnp.concatenate([-w[..., half:], w[..., :half]], -1)


def _prep_layer_weights(l, w_in, rwkv_wup, rwkv_aup, mla_wuq, mla_wukv):
    D = D_MODEL
    wi = w_in[l]
    ret_rwkv = wi[:, :7168]
    lora = wi[:, 7168:7296]
    dg = wi[:, 7296:7424]
    qc = wi[:, 7424:7936]
    kvc = wi[:, 7936:8192]
    kr = wi[:, 8192:8256]
    w_u = jnp.concatenate([ret_rwkv, qc, kvc, kr, _rot_half_cols(kr), lora, dg,
                           jnp.zeros((D, U_COLS - 8320), F32)], -1).astype(BF16)

    z64 = jnp.zeros((2, 64, RWKV_W), F32)
    wup = jnp.concatenate([rwkv_wup[l], z64], 1)
    aup = jnp.concatenate([z64, rwkv_aup[l]], 1)

    wq = mla_wuq[l].reshape(MLA_Q_RANK, MLA_HEADS, MLA_NOPE + MLA_ROPE)
    wq_r = wq[..., MLA_NOPE:]
    wq = jnp.concatenate([wq[..., :MLA_NOPE], wq_r, _rot_half_cols(wq_r)], -1)
    wq = wq.reshape(MLA_Q_RANK, MLA_HEADS * MLA_SLOT).astype(BF16)

    wkv = mla_wukv[l].reshape(MLA_KV_RANK, MLA_HEADS, MLA_NOPE + MLA_V)
    wk_top = jnp.concatenate([wkv[..., :MLA_NOPE], jnp.zeros((MLA_KV_RANK, MLA_HEADS, 2 * MLA_ROPE), F32)], -1)
    eye = jnp.eye(2 * MLA_ROPE, dtype=F32)[:, None, :]
    wk_bot = jnp.concatenate([jnp.zeros((2 * MLA_ROPE, MLA_HEADS, MLA_NOPE), F32),
                              jnp.broadcast_to(eye, (2 * MLA_ROPE, MLA_HEADS, 2 * MLA_ROPE))], -1)
    wk = jnp.concatenate([wk_top, wk_bot], 0).reshape(MLA_KV_RANK + 2 * MLA_ROPE, MLA_HEADS * MLA_SLOT).astype(BF16)
    wv = wkv[..., MLA_NOPE:].reshape(MLA_KV_RANK, MLA_HEADS * MLA_V).astype(BF16)
    return w_u, wup, aup, wq, wk, wv


def _rope_tables(T):
    pos = jnp.arange(T, dtype=F32)

    def cs(half):
        inv = ROPE_THETA ** (-jnp.arange(half, dtype=F32) / half)
        ang = pos[:, None] * inv[None, :]
        return jnp.cos(ang), jnp.sin(ang)

    ret_cos, ret_sin = cs(RET_DK // 2)
    c, s = cs(MLA_ROPE // 2)
    ones = jnp.ones((T, MLA_NOPE), F32)
    zn = jnp.zeros((T, MLA_NOPE), F32)
    zr = jnp.zeros((T, MLA_ROPE), F32)
    ct = jnp.concatenate([ones, c, c, zr], -1)
    st = jnp.concatenate([zn, s, s, zr], -1)
    scale = (MLA_NOPE + MLA_ROPE) ** -0.5
    return ret_cos, ret_sin, ct * scale, st * scale, ct, st


def _trunk(x, ada, tables, lw, consts):
    B, T, D = x.shape
    ret_cos, ret_sin, ctq, stq, ctk, stk = tables
    lg_f, lg_b, ones_bd = consts
    tm_prep = min(T, 256)
    for l in range(DEPTH):
        (w_u, w_mg, b_mg, wup, aup, wq, wk, wv, p) = lw[l]
        sh1, sc1, g1, sh2, sc2, g2 = [ada[l][:, None, i * D:(i + 1) * D] for i in range(6)]
        zero_b = jnp.zeros((1, U_COLS), F32)
        u = _inproj(x, sc1, sh1, w_u, zero_b, gate=False, tn=768, name="in_proj")
        gates = _inproj(x, sc1, sh1, w_mg, b_mg, gate=True, tn=1024, name="merge_gates")

        rf = _retention(u, ret_cos, ret_sin, lg_f, rev=False)
        rb = _retention(u, ret_cos, ret_sin, lg_b, rev=True)

        cols = jnp.concatenate([u[:, tm_prep - 1::tm_prep, 4096:7168], u[:, tm_prep - 1::tm_prep, U_LORA:U_LORA + LANES]], -1)
        zrow = jnp.zeros((B, 1, cols.shape[-1]), F32)
        prev_rows = jnp.concatenate([zrow, cols[:, :-1]], 1)[:, :, None, :]
        cols_n = jnp.concatenate([u[:, tm_prep::tm_prep, 4096:7168], u[:, tm_prep::tm_prep, U_LORA:U_LORA + LANES]], -1)
        next_rows = jnp.concatenate([cols_n, zrow], 1)[:, :, None, :]
        outs = _rwkv_prep(u, prev_rows, next_rows, p["mu"], p["w0"], p["a0"], wup, aup, p["gup"], p["kkg"], p["kag"],
                          tm=tm_prep)
        yf, bf = _rwkv_scan(*outs[0:6], p["rk"], rev=False)
        yb, bb = _rwkv_scan(*outs[6:12], p["rk"], rev=True)
        wg = outs[12]

        q, k, v = _mla_prep(u, p["gq"], p["gkv"], wq, wk, wv, ctq, stq, ctk, stk)
        attn = _flash(q, k, v)

        x = _mix_out(rf, rb, u, yf, yb, bf, bb, wg, attn, gates, x, g1, p["gn"], p["lxg"], p["lxb"],
                     p["ln1g"], p["ln1b"], ones_bd, p["wa"], p["wb"], p["wc"], p["wo"])
        x = _mlp(x, sc2, sh2, g2, p["w1"], p["w2"], p["ln2g"], p["ln2b"])
    return x


def kernel(x_prompt, x_sample, c_prompt, c_sample, w_ada, b_ada, w_in, ret_gn, ret_wo, rwkv_mu, rwkv_w0, rwkv_wup, rwkv_a0, rwkv_aup, rwkv_gup, rwkv_kk, rwkv_ka, rwkv_rk, rwkv_lnx_g, rwkv_lnx_b, rwkv_wo, mla_qnorm, mla_wuq, mla_kvnorm, mla_wukv, mla_wo, w_merge, b_merge, w_out, ln_g, ln_b, w_mlp1, w_mlp2):
    D = D_MODEL
    nb_p = c_prompt.shape[0]
    nb_s = c_sample.shape[0]
    c_pad = jnp.concatenate([c_prompt, c_sample, jnp.zeros((8 - nb_p - nb_s, D), F32)], 0)
    ada = _ada_all(c_pad, w_ada, b_ada)

    lw = []
    for l in range(DEPTH):
        w_u, wup, aup, wq, wk, wv = _prep_layer_weights(l, w_in, rwkv_wup, rwkv_aup, mla_wuq, mla_wukv)
        p = dict(
            mu=rwkv_mu[l][:, None, :], w0=rwkv_w0[l][:, None, :], a0=rwkv_a0[l][:, None, :],
            gup=rwkv_gup[l], kkg=rwkv_kk[l][None], kag=rwkv_ka[l][None], rk=rwkv_rk[l][None],
            gq=mla_qnorm[l][None], gkv=mla_kvnorm[l][None], gn=ret_gn[l][None],
            lxg=rwkv_lnx_g[l][None], lxb=rwkv_lnx_b[l][None],
            ln1g=ln_g[l, 0][None], ln1b=ln_b[l, 0][None], ln2g=ln_g[l, 1][None], ln2b=ln_b[l, 1][None],
            wa=ret_wo[l].astype(BF16), wb=rwkv_wo[l].astype(BF16), wc=mla_wo[l].astype(BF16),
            wo=w_out[l].astype(BF16), w1=w_mlp1[l].astype(BF16), w2=w_mlp2[l].astype(BF16),
        )
        lw.append((w_u, w_merge[l].astype(BF16), b_merge[l][None], wup, aup, wq, wk, wv, p))

    log_g = jnp.log1p(-jnp.exp2(-5.0 - jnp.arange(RET_HEADS, dtype=F32)))
    head = jnp.arange(RWKV_W) // RWKV_HEAD
    ones_bd = (head[:, None] == head[None, :]).astype(BF16)
    consts = (log_g, log_g[::-1], ones_bd)

    y_p = _trunk(x_prompt, ada[:, :nb_p], _rope_tables(x_prompt.shape[1]), lw, consts)
    y_s = _trunk(x_sample, ada[:, nb_p:nb_p + nb_s], _rope_tables(x_sample.shape[1]), lw, consts)
    return (y_p, y_s)
```

```python
import functools
import math

import jax
import jax.numpy as jnp
from jax import lax
from jax.experimental import pallas as pl
from jax.experimental.pallas import tpu as pltpu

F32 = jnp.float32
BF16 = jnp.bfloat16
HI = lax.Precision.HIGHEST

D_MODEL = 1024
DEPTH = 4
RET_HEADS = 4
RET_DK = 256
RET_CHUNK = 128
RWKV_HEAD = 64
RWKV_HEADS = 16
RWKV_W = 1024
RWKV_PAIRS = RWKV_HEADS // 2
RWKV_CHUNK = 64
MLA_HEADS = 8
MLA_NOPE = 128
MLA_ROPE = 64
MLA_V = 128
MLA_Q_RANK = 512
MLA_KV_RANK = 256
MLA_SLOT = 256
D_FF = 4 * D_MODEL
ROPE_THETA = 10000.0
ALPHA = (2 * DEPTH) ** 0.25
EPS = 1e-5
LANES = 128

U_RET_G = 3
U_RWKV_R = 4
U_QC = 7168
U_KV = 7680
U_LORA = 8064
U_DG = 8192
U_COLS = 8448

VMEM_LIMIT = 56 * 1024 * 1024


def _cparams(sem):
    return pltpu.CompilerParams(dimension_semantics=sem, vmem_limit_bytes=VMEM_LIMIT)


def _dot(a, b, precision=None):
    return jnp.dot(a, b, preferred_element_type=F32, precision=precision)


def _dot_nt(a, b, precision=None):
    return lax.dot_general(a, b, (((1,), (1,)), ((), ())), preferred_element_type=F32, precision=precision)


def _dot_tn(a, b, precision=None):
    return lax.dot_general(a, b, (((0,), (0,)), ((), ())), preferred_element_type=F32, precision=precision)


def _split_bf16(x, n):
    parts = []
    for _ in range(n):
        p = x.astype(BF16)
        parts.append(p)
        x = x - p.astype(F32)
    return parts


def _sigmoid(x):
    return 1.0 / (1.0 + jnp.exp(-x))


def _layer_norm(x, g, b):
    mu = jnp.mean(x, -1, keepdims=True)
    xc = x - mu
    var = jnp.mean(xc * xc, -1, keepdims=True)
    return xc * lax.rsqrt(var + EPS) * g + b


def _ada_kernel(c_ref, w_ref, b_ref, o_ref):
    c = c_ref[...]
    o_ref[0] = _dot(c * _sigmoid(c), w_ref[0], HI) + b_ref[0]


def _ada_all(c_pad, w_ada, b_ada):
    L, D, N = w_ada.shape
    tn = 1536
    return pl.pallas_call(
        _ada_kernel,
        grid=(L, N // tn),
        in_specs=[
            pl.BlockSpec((8, D), lambda l, j: (0, 0)),
            pl.BlockSpec((1, D, tn), lambda l, j: (l, 0, j)),
            pl.BlockSpec((1, 1, tn), lambda l, j: (l, 0, j)),
        ],
        out_specs=pl.BlockSpec((1, 8, tn), lambda l, j: (l, 0, j)),
        out_shape=jax.ShapeDtypeStruct((L, 8, N), F32),
        compiler_params=_cparams(("parallel", "parallel")),
        name="ada",
    )(c_pad, w_ada, b_ada.reshape(L, 1, N))


def _inproj_kernel(x_ref, sc_ref, sh_ref, w_ref, b_ref, o_ref, h_ref, *, gate):
    @pl.when(pl.program_id(2) == 0)
    def _():
        h_ref[...] = (x_ref[0] * (1.0 + sc_ref[0]) + sh_ref[0]).astype(BF16)

    acc = _dot(h_ref[...], w_ref[...])
    if gate:
        acc = _sigmoid(acc + b_ref[...])
    o_ref[0] = acc.astype(o_ref.dtype)


def _inproj(x, sc, sh, w, b, *, gate, tn, name):
    B, T, D = x.shape
    N = w.shape[1]
    tm = min(T, 1024)
    return pl.pallas_call(
        functools.partial(_inproj_kernel, gate=gate),
        grid=(B, T // tm, N // tn),
        in_specs=[
            pl.BlockSpec((1, tm, D), lambda b_, i, j: (b_, i, 0)),
            pl.BlockSpec((1, 1, D), lambda b_, i, j: (b_, 0, 0)),
            pl.BlockSpec((1, 1, D), lambda b_, i, j: (b_, 0, 0)),
            pl.BlockSpec((D, tn), lambda b_, i, j: (0, j)),
            pl.BlockSpec((1, tn), lambda b_, i, j: (0, j)),
        ],
        out_specs=pl.BlockSpec((1, tm, tn), lambda b_, i, j: (b_, i, j)),
        out_shape=jax.ShapeDtypeStruct((B, T, N), F32),
        scratch_shapes=[pltpu.VMEM((tm, D), BF16)],
        compiler_params=_cparams(("parallel", "parallel", "arbitrary")),
        name=name,
    )(x, sc, sh, w, b)


def _ret_kernel(lg_ref, q_ref, k_ref, v_ref, cos_ref, sin_ref, o_ref, state_ref, *, rev, tb):
    C = RET_CHUNK
    half = RET_DK // 2

    @pl.when(pl.program_id(2) == 0)
    def _():
        state_ref[...] = jnp.zeros_like(state_ref)

    lg = lg_ref[pl.program_id(1)]
    row = lax.broadcasted_iota(jnp.int32, (C, C), 0).astype(F32)
    col = lax.broadcasted_iota(jnp.int32, (C, C), 1).astype(F32)
    idx = lax.broadcasted_iota(jnp.int32, (C, 1), 0).astype(F32)
    if rev:
        dist = col - row
        decay_in = jnp.where(dist > 0, jnp.exp(lg * jnp.maximum(dist, 0.0)), 0.0)
        q_scale = jnp.exp(lg * (C - idx))
        k_scale = jnp.exp(lg * idx)
    else:
        dist = row - col
        decay_in = jnp.where(dist >= 0, jnp.exp(lg * jnp.maximum(dist, 0.0)), 0.0)
        q_scale = jnp.exp(lg * (idx + 1.0))
        k_scale = jnp.exp(lg * (C - 1.0 - idx))
    chunk_decay = jnp.exp(lg * C)

    def rot(x, cos, sin):
        x1, x2 = x[:, :half], x[:, half:]
        return jnp.concatenate([x1 * cos - x2 * sin, x1 * sin + x2 * cos], -1)

    n_chunks = tb // C
    state = state_ref[...]
    for cc in range(n_chunks):
        c = n_chunks - 1 - cc if rev else cc
        rows = slice(c * C, (c + 1) * C)
        cos = cos_ref[rows, :]
        sin = sin_ref[rows, :]
        q = rot(q_ref[0, rows, :], cos, sin).astype(BF16)
        k = rot(k_ref[0, rows, :], cos, sin) * (RET_DK ** -0.5)
        v = v_ref[0, rows, :].astype(BF16)
        s = _dot_nt(q, k.astype(BF16)) * decay_in
        o_ref[0, rows, :] = _dot(s.astype(BF16), v) + _dot(q, state.astype(BF16)) * q_scale
        state = state * chunk_decay + _dot_tn((k * k_scale).astype(BF16), v)
    state_ref[...] = state


def _retention(u, cos, sin, lg, *, rev):
    B, T, _ = u.shape
    tb = min(T, 512)
    nt = T // tb
    tmap = (lambda i: nt - 1 - i) if rev else (lambda i: i)
    return pl.pallas_call(
        functools.partial(_ret_kernel, rev=rev, tb=tb),
        grid=(B, RET_HEADS, nt),
        in_specs=[
            pl.BlockSpec(memory_space=pltpu.SMEM),
            pl.BlockSpec((1, tb, RET_DK), lambda b_, h, i: (b_, tmap(i), h)),
            pl.BlockSpec((1, tb, RET_DK), lambda b_, h, i: (b_, tmap(i), RET_HEADS + h)),
            pl.BlockSpec((1, tb, RET_DK), lambda b_, h, i: (b_, tmap(i), 2 * RET_HEADS + h)),
            pl.BlockSpec((tb, RET_DK // 2), lambda b_, h, i: (tmap(i), 0)),
            pl.BlockSpec((tb, RET_DK // 2), lambda b_, h, i: (tmap(i), 0)),
        ],
        out_specs=pl.BlockSpec((1, tb, RET_DK), lambda b_, h, i: (b_, tmap(i), h)),
        out_shape=jax.ShapeDtypeStruct((B, T, RET_HEADS * RET_DK), F32),
        scratch_shapes=[pltpu.VMEM((RET_DK, RET_DK), F32)],
        compiler_params=_cparams(("parallel", "parallel", "arbitrary")),
        name="ret_bwd" if rev else "ret_fwd",
    )(lg, u, u, u, cos, sin)


def _rwkv_prep_kernel(r_ref, k_ref, v_ref, l_ref, dg_ref, prev_ref, next_ref, mu_ref, w0_ref, a0_ref,
                      wup_ref, aup_ref, gup_ref, kkg_ref, kag_ref, *out_refs, tm):
    g_ref = out_refs[-1]
    W = RWKV_W
    xs = (r_ref[0], k_ref[0], v_ref[0], l_ref[0])
    offs = (0, W, 2 * W, 3 * W)
    row = lax.broadcasted_iota(jnp.int32, (tm, 1), 0)
    lane = lax.broadcasted_iota(jnp.int32, (tm, LANES), 1)
    for d in range(2):
        if d == 0:
            nb, edge, shift = prev_ref[0, 0], 0, 1
        else:
            nb, edge, shift = next_ref[0, 0], tm - 1, tm - 1
        mu = mu_ref[d]
        xd = []
        for x, off in zip(xs, offs):
            w = x.shape[1]
            shifted = jnp.where(row == edge, nb[:, off:off + w], pltpu.roll(x, shift, 0))
            xd.append(x + (shifted - x) * mu[:, off:off + w])
        xr, xk, xv, xl = xd
        lhs = jnp.where(lane < 64, jnp.tanh(xl), xl)
        lw = -math.exp(-0.5) * _sigmoid(w0_ref[d] + _dot(lhs, wup_ref[d], HI))
        a = _sigmoid(a0_ref[d] + _dot(lhs, aup_ref[d], HI))
        o = out_refs[6 * d:6 * d + 6]
        o[0][0] = xr
        o[1][0] = lw
        o[2][0] = xk * (1.0 + (a - 1.0) * kag_ref[...])
        o[3][0] = xv
        o[4][0] = xk * kkg_ref[...]
        o[5][0] = a
    g_ref[0] = _dot(_sigmoid(dg_ref[0]), gup_ref[...], HI)


def _rwkv_prep(u, prev_rows, next_rows, mu, w0, a0, wup, aup, gup, kkg, kag, *, tm):
    B, T, _ = u.shape
    W = RWKV_W
    full = lambda shape: pl.BlockSpec(shape, lambda b_, i: (0,) * len(shape))
    ublk = lambda width, cb: pl.BlockSpec((1, tm, width), lambda b_, i: (b_, i, cb))
    edge = pl.BlockSpec((1, 1, 1, 3 * W + LANES), lambda b_, i: (b_, i, 0, 0))
    out = jax.ShapeDtypeStruct((B, T, W), F32)
    return pl.pallas_call(
        functools.partial(_rwkv_prep_kernel, tm=tm),
        grid=(B, T // tm),
        in_specs=[
            ublk(W, U_RWKV_R), ublk(W, U_RWKV_R + 1), ublk(W, U_RWKV_R + 2),
            ublk(LANES, U_LORA // LANES), ublk(LANES, U_DG // LANES),
            edge, edge,
            full((2, 1, 3 * W + LANES)), full((2, 1, W)), full((2, 1, W)),
            full((2, LANES, W)), full((2, LANES, W)), full((LANES, W)),
            full((1, W)), full((1, W)),
        ],
        out_specs=[pl.BlockSpec((1, tm, W), lambda b_, i: (b_, i, 0))] * 13,
        out_shape=[out] * 13,
        compiler_params=_cparams(("parallel", "parallel")),
        name="rwkv_prep",
    )(u, u, u, u, u, prev_rows, next_rows, mu, w0, a0, wup, aup, gup, kkg, kag)


def _rwkv_kernel(r_ref, lw_ref, ke_ref, v_ref, kk_ref, a_ref, rk_ref, y_ref, bon_ref,
                 state_ref, p_ref, q_ref, ry_ref, y1_ref, et_ref, bs_ref, *, rev, tb):
    C = RWKV_CHUNK
    N = 2 * C

    @pl.when(pl.program_id(2) == 0)
    def _():
        state_ref[...] = jnp.zeros_like(state_ref)
        for ref in (p_ref, q_ref, ry_ref, y1_ref, et_ref, bs_ref):
            ref[1] = jnp.zeros(ref.shape[1:], ref.dtype)

    rowi = lax.broadcasted_iota(jnp.int32, (N, N), 0)
    coli = lax.broadcasted_iota(jnp.int32, (N, N), 1)
    same_head = (rowi // C) == (coli // C)
    tr, tc = rowi % C, coli % C
    before = (tc > tr) if rev else (tc < tr)
    strict = (same_head & before).astype(F32)
    incl = (same_head & (before | (tc == tr))).astype(F32)
    eye = (rowi == coli).astype(F32)
    head_mask = same_head.astype(F32)
    head_ones = same_head.astype(BF16)
    level_masks = []
    b = 1
    while b < C:
        level_masks.append(((rowi // (2 * b) == coli // (2 * b)) & (rowi // b != coli // b)).astype(F32))
        b *= 2
    ci = lax.broadcasted_iota(jnp.int32, (C, C), 0)
    cj = lax.broadcasted_iota(jnp.int32, (C, C), 1)
    cum_mat = ((cj >= ci) if rev else (cj <= ci)).astype(BF16)
    rk = rk_ref[...]

    def stack(x):
        return jnp.concatenate([x, x], axis=0) * head_mask

    def head_sum(x):
        return sum(_dot(p, head_ones) for p in _split_bf16(x, 2))

    nc = tb // C
    order = [nc - 1 - j if rev else j for j in range(nc)]
    every = lambda f, *lists: [f(*args) for args in zip(*lists)]

    i = pl.program_id(2)
    slot = i % 2
    prev = 1 - slot

    carry = [state_ref[...]]

    def recur(j):
        c = order[j]
        s = carry[0]
        sb = s.astype(BF16)
        y = _dot_nt(ry_ref[prev, c], sb) + y1_ref[prev, c]
        y_ref[0, c * C:(c + 1) * C, :] = y[:C] + y[C:]
        carry[0] = s * et_ref[prev, c][0:1] + _dot(sb, p_ref[prev, c]) + q_ref[prev, c]

    kk_all = kk_ref[0]
    r_all, ke_all, v_all = r_ref[0], ke_ref[0], v_ref[0]
    kk_all = kk_all * lax.rsqrt(head_sum(kk_all * kk_all) + 1e-12)
    bs_ref[slot] = head_sum(r_all * ke_all * rk) * v_all
    beta_all = kk_all * a_ref[0]
    lw_all = lw_ref[0]
    rows = [slice(c * C, (c + 1) * C) for c in order]
    lw_parts = _split_bf16(lw_all, 3)
    cum = [sum(_dot(cum_mat, p[rw]) for p in lw_parts) for rw in rows]
    recur(0)
    tot = [cm[0:1] if rev else cm[C - 1:C] for cm in cum]
    e_neg = every(lambda cm: jnp.exp(-cm), cum)
    e_end = every(lambda cm, t: jnp.exp(t - cm), cum, tot)
    a_st = every(lambda rw, cm: stack(-kk_all[rw] * jnp.exp(cm - lw_all[rw])).astype(BF16), rows, cum)
    r_stf = every(lambda rw, cm: stack(r_all[rw] * jnp.exp(cm)), rows, cum)
    r_st = every(lambda x: x.astype(BF16), r_stf)
    b_st = every(lambda rw, e: stack(beta_all[rw] * e).astype(BF16), rows, e_neg)
    k_st = every(lambda rw, e: stack(ke_all[rw] * e).astype(BF16), rows, e_neg)
    v_st = every(lambda rw: stack(v_all[rw]).astype(BF16), rows)
    bh_st = every(lambda rw, e: stack(beta_all[rw] * e).astype(BF16), rows, e_end)
    kh_st = every(lambda rw, e: stack(ke_all[rw] * e).astype(BF16), rows, e_end)

    gram = every(lambda a_, r_, b_, k_: _dot_nt(jnp.concatenate([a_, r_], 0), jnp.concatenate([b_, k_], 0)),
                 a_st, r_st, b_st, k_st)
    recur(1)
    m_ab = every(lambda g: g[:N, :N] * strict, gram)
    m_ak = every(lambda g: (g[:N, N:] * strict).astype(BF16), gram)
    n_rb = every(lambda g: (g[N:, :N] * incl).astype(BF16), gram)
    n_rk = every(lambda g: (g[N:, N:] * incl).astype(BF16), gram)

    x = every(lambda m: eye + m * level_masks[0], m_ab)
    for lvl, mask in enumerate(level_masks[1:]):
        xb = every(lambda x_: x_.astype(BF16), x)
        t = every(lambda xb_, m: _dot(xb_, (m * mask).astype(BF16)).astype(BF16), xb, m_ab)
        x = every(lambda x_, t_, xb_: x_ + _dot(t_, xb_), x, t, xb)
        recur(2 + lvl)
    xb = every(lambda x_: x_.astype(BF16), x)

    z = every(lambda m, v_: _dot(m, v_).astype(BF16), m_ak, v_st)
    w12 = every(lambda xb_, a_, z_: _dot(xb_, jnp.concatenate([a_, z_], 1)).astype(BF16), xb, a_st, z)
    recur(nc - 1)
    state_ref[...] = carry[0]
    nw = every(_dot, n_rb, w12)
    y0 = every(_dot, n_rk, v_st)
    bw = every(_dot_tn, w12, bh_st)
    kv = every(_dot_tn, v_st, kh_st)
    for j, c in enumerate(order):
        ry_ref[slot, c] = (r_stf[j] + nw[j][:, :N]).astype(BF16)
        y1_ref[slot, c] = nw[j][:, N:] + y0[j]
        p_ref[slot, c] = bw[j][:N].astype(BF16)
        q_ref[slot, c] = bw[j][N:] + kv[j]
        et_ref[slot, c] = jnp.broadcast_to(jnp.exp(tot[j]), (8, N))
    bon_ref[0] = bs_ref[prev]


def _rwkv_scan(r, lw, ke, v, kk, a, rk, *, rev):
    B, T, W = r.shape
    tb = min(T, 512)
    nt = T // tb
    nc = tb // RWKV_CHUNK
    assert nc == 8, "the recurrence steps are placed between 8 preparation stages"
    tmap = (lambda i: nt - 1 - i) if rev else (lambda i: i)
    blk_in = pl.BlockSpec((1, tb, LANES), lambda b_, p, i: (b_, tmap(jnp.minimum(i, nt - 1)), p))
    blk_out = pl.BlockSpec((1, tb, LANES), lambda b_, p, i: (b_, tmap(jnp.maximum(i - 1, 0)), p))
    out = jax.ShapeDtypeStruct((B, T, W), F32)
    return pl.pallas_call(
        functools.partial(_rwkv_kernel, rev=rev, tb=tb),
        grid=(B, RWKV_PAIRS, nt + 1),
        in_specs=[blk_in] * 6 + [pl.BlockSpec((1, LANES), lambda b_, p, i: (0, p))],
        out_specs=[blk_out, blk_out],
        out_shape=[out, out],
        scratch_shapes=[
            pltpu.VMEM((LANES, LANES), F32),
            pltpu.VMEM((2, nc, LANES, LANES), BF16), pltpu.VMEM((2, nc, LANES, LANES), F32),
            pltpu.VMEM((2, nc, LANES, LANES), BF16), pltpu.VMEM((2, nc, LANES, LANES), F32),
            pltpu.VMEM((2, nc, 8, LANES), F32), pltpu.VMEM((2, tb, LANES), F32),
        ],
        compiler_params=_cparams(("parallel", "parallel", "arbitrary")),
        name="rwkv_bwd" if rev else "rwkv_fwd",
    )(r, lw, ke, v, kk, a, rk)


def _mla_prep_kernel(qc_ref, kv_ref, gq_ref, gkv_ref, wq_ref, wk_ref, wv_ref, ctq_ref, stq_ref, ctk_ref,
                     stk_ref, vt_ref, q_ref, k_ref, v_ref):
    def rms(x, g):
        return x * lax.rsqrt(jnp.mean(x * x, -1, keepdims=True) + EPS) * g

    qn = rms(qc_ref[0], gq_ref[...]).astype(BF16)
    kvx = kv_ref[0]
    kvn = rms(kvx[:, :MLA_KV_RANK], gkv_ref[...])
    kin = jnp.concatenate([kvn, kvx[:, MLA_KV_RANK:]], -1).astype(BF16)
    q = _dot(qn, wq_ref[...])
    k = _dot(kin, wk_ref[...])
    v_ref[0] = (_dot(kvn.astype(BF16), wv_ref[...]) + vt_ref[...]).astype(BF16)
    ctq, stq, ctk, stk = ctq_ref[...], stq_ref[...], ctk_ref[...], stk_ref[...]
    for h in range(MLA_HEADS):
        sl = slice(h * MLA_SLOT, (h + 1) * MLA_SLOT)
        qh, kh = q[:, sl], k[:, sl]
        q_ref[0, :, sl] = (qh * ctq + pltpu.roll(qh, MLA_SLOT - MLA_ROPE, 1) * stq).astype(BF16)
        k_ref[0, :, sl] = (kh * ctk + pltpu.roll(kh, MLA_SLOT - MLA_ROPE, 1) * stk).astype(BF16)


def _mla_prep(u, gq, gkv, wq, wk, wv, ctq, stq, ctk, stk, vt):
    B, T, _ = u.shape
    tm = min(T, 512)
    kvw = MLA_KV_RANK + 2 * MLA_ROPE
    QW = MLA_HEADS * MLA_SLOT
    full = lambda shape: pl.BlockSpec(shape, lambda b_, i: (0,) * len(shape))
    tab = pl.BlockSpec((tm, MLA_SLOT), lambda b_, i: (i, 0))
    tok = pl.BlockSpec((1, tm, QW), lambda b_, i: (b_, i, 0))
    return pl.pallas_call(
        _mla_prep_kernel,
        grid=(B, T // tm),
        in_specs=[
            pl.BlockSpec((1, tm, MLA_Q_RANK), lambda b_, i: (b_, i, U_QC // MLA_Q_RANK)),
            pl.BlockSpec((1, tm, kvw), lambda b_, i: (b_, i, U_KV // kvw)),
            full((1, MLA_Q_RANK)), full((1, MLA_KV_RANK)),
            full((MLA_Q_RANK, QW)), full((kvw, QW)), full((MLA_KV_RANK, QW)),
            tab, tab, tab, tab, full((1, QW)),
        ],
        out_specs=[tok, tok, tok],
        out_shape=[jax.ShapeDtypeStruct((B, T, QW), BF16)] * 3,
        compiler_params=_cparams(("parallel", "parallel")),
        name="mla_prep",
    )(u, u, gq, gkv, wq, wk, wv, ctq, stq, ctk, stk, vt)


def _flash_kernel(q_ref, k_ref, v_ref, o_ref, *, tkc, n_kv, unroll):
    q = q_ref[0]
    tq = q.shape[0]

    def step(j, carry):
        m, acc = carry
        off = pl.multiple_of(j * tkc, tkc)
        s = _dot_nt(q, k_ref[0, pl.ds(off, tkc), :])
        m_new = jnp.maximum(m, jnp.max(s, -1, keepdims=True))
        p = jnp.exp2(s - m_new)
        acc = jnp.exp2(m - m_new) * acc + _dot(p.astype(BF16), v_ref[0, pl.ds(off, tkc), :])
        return m_new, acc

    init = (jnp.full((tq, 1), -jnp.inf, F32), jnp.zeros((tq, MLA_SLOT), F32))
    _, acc = lax.fori_loop(0, n_kv, step, init, unroll=unroll)
    o_ref[0] = (acc[:, :MLA_V] / acc[:, MLA_V:]).astype(o_ref.dtype)


def _flash(q, k, v):
    B, T, _ = q.shape
    tq = min(T, 512)
    tkc = min(T, 512)
    n_kv = T // tkc
    resident = lambda: pl.BlockSpec((1, T, MLA_SLOT), lambda b_, h, i: (b_, 0, h), pipeline_mode=pl.Buffered(1))
    return pl.pallas_call(
        functools.partial(_flash_kernel, tkc=tkc, n_kv=n_kv, unroll=min(n_kv, 4)),
        grid=(B, MLA_HEADS, T // tq),
        in_specs=[pl.BlockSpec((1, tq, MLA_SLOT), lambda b_, h, i: (b_, i, h)), resident(), resident()],
        out_specs=pl.BlockSpec((1, tq, MLA_V), lambda b_, h, i: (b_, i, h)),
        out_shape=jax.ShapeDtypeStruct((B, T, MLA_HEADS * MLA_V), BF16),
        compiler_params=_cparams(("parallel", "parallel", "arbitrary")),
        name="mla_attn",
    )(q, k, v)


def _seg_sum(x, ones_bd):
    return sum(_dot(p, ones_bd) for p in _split_bf16(x, 2))


def _mix_out_kernel(rf_ref, rb_ref, rg_ref, yf_ref, yb_ref, bf_ref, bb_ref, wg_ref, at_ref, gt_ref, x_ref,
                    g1_ref, gn_ref, lxg_ref, lxb_ref, lng_ref, lnb_ref, ones_ref,
                    wa_ref, wb_ref, wc_ref, wo_ref, o_ref):
    D = D_MODEL
    ret = rf_ref[0] + rb_ref[0]
    parts = []
    for h in range(RET_HEADS):
        xh = ret[:, h * RET_DK:(h + 1) * RET_DK]
        mu = jnp.mean(xh, -1, keepdims=True)
        xc = xh - mu
        var = jnp.mean(xc * xc, -1, keepdims=True)
        parts.append(xc * lax.rsqrt(var + EPS))
    rg = rg_ref[0]
    a_in = jnp.concatenate(parts, -1) * gn_ref[...] * (rg * _sigmoid(rg))
    y_a = _dot(a_in.astype(BF16), wa_ref[...])

    y = yf_ref[0] + yb_ref[0]
    ones_bd = ones_ref[...]
    mu = _seg_sum(y, ones_bd) * (1.0 / RWKV_HEAD)
    yc = y - mu
    var = _seg_sum(yc * yc, ones_bd) * (1.0 / RWKV_HEAD)
    o_b = yc * lax.rsqrt(var + EPS) * lxg_ref[...] + lxb_ref[...] + bf_ref[0] + bb_ref[0]
    y_b = _dot((o_b * wg_ref[0]).astype(BF16), wb_ref[...])

    y_c = _dot(at_ref[0], wc_ref[...])

    gt = gt_ref[0]
    merged = gt[:, :D] * y_a + gt[:, D:2 * D] * y_b + gt[:, 2 * D:] * y_c
    mix = _dot(merged.astype(BF16), wo_ref[...])
    o_ref[0] = _layer_norm(ALPHA * x_ref[0] + (1.0 + g1_ref[0]) * mix, lng_ref[...], lnb_ref[...])


def _mix_out(rf, rb, u, yf, yb, bf, bb, wg, attn, gates, x, g1, gn, lxg, lxb, lng, lnb, ones_bd, wa, wb, wc, wo):
    B, T, D = x.shape
    tm = min(T, 256)
    tok = lambda width, cb=0: pl.BlockSpec((1, tm, width), lambda b_, i: (b_, i, cb))
    vec = pl.BlockSpec((1, D), lambda b_, i: (0, 0))
    mat = pl.BlockSpec((D, D), lambda b_, i: (0, 0))
    return pl.pallas_call(
        _mix_out_kernel,
        grid=(B, T // tm),
        in_specs=[tok(D), tok(D), tok(D, U_RET_G), tok(D), tok(D), tok(D), tok(D), tok(D), tok(D), tok(3 * D),
                  tok(D), pl.BlockSpec((1, 1, D), lambda b_, i: (b_, 0, 0)),
                  vec, vec, vec, vec, vec, mat, mat, mat, mat, mat],
        out_specs=tok(D),
        out_shape=jax.ShapeDtypeStruct((B, T, D), F32),
        compiler_params=_cparams(("parallel", "parallel")),
        name="mix_out",
    )(rf, rb, u, yf, yb, bf, bb, wg, attn, gates, x, g1, gn, lxg, lxb, lng, lnb, ones_bd, wa, wb, wc, wo)


def _mlp_kernel(x_ref, sc_ref, sh_ref, g2_ref, w1_ref, w2_ref, lng_ref, lnb_ref, o_ref, h_ref, acc_ref):
    f = pl.program_id(2)

    @pl.when(f == 0)
    def _():
        h_ref[...] = (x_ref[0] * (1.0 + sc_ref[0]) + sh_ref[0]).astype(BF16)
        acc_ref[...] = jnp.zeros_like(acc_ref)

    a = jnp.maximum(_dot(h_ref[...], w1_ref[...]), 0.0)
    acc_ref[...] += _dot((a * a).astype(BF16), w2_ref[...])

    @pl.when(f == pl.num_programs(2) - 1)
    def _():
        o_ref[0] = _layer_norm(ALPHA * x_ref[0] + (1.0 + g2_ref[0]) * acc_ref[...], lng_ref[...], lnb_ref[...])


def _mlp(x, sc, sh, g2, w1, w2, lng, lnb):
    B, T, D = x.shape
    F = w1.shape[1]
    tm = min(T, 1024)
    tf = 1024
    ada = pl.BlockSpec((1, 1, D), lambda b_, i, f: (b_, 0, 0))
    vec = pl.BlockSpec((1, D), lambda b_, i, f: (0, 0))
    return pl.pallas_call(
        _mlp_kernel,
        grid=(B, T // tm, F // tf),
        in_specs=[
            pl.BlockSpec((1, tm, D), lambda b_, i, f: (b_, i, 0)), ada, ada, ada,
            pl.BlockSpec((D, tf), lambda b_, i, f: (0, f)),
            pl.BlockSpec((tf, D), lambda b_, i, f: (f, 0)),
            vec, vec,
        ],
        out_specs=pl.BlockSpec((1, tm, D), lambda b_, i, f: (b_, i, 0)),
        out_shape=jax.ShapeDtypeStruct((B, T, D), F32),
        scratch_shapes=[pltpu.VMEM((tm, D), BF16), pltpu.VMEM((tm, D), F32)],
        compiler_params=_cparams(("parallel", "parallel", "arbitrary")),
        name="mlp",
    )(x, sc, sh, g2, w1, w2, lng, lnb)


def _rot_half_cols(w):
    half = w.shape[-1] // 2
    return jnp.concatenate([-w[..., half:], w[..., :half]], -1)


def _prep_layer_weights(l, w_in, rwkv_wup, rwkv_aup, mla_wuq, mla_wukv):
    D = D_MODEL
    wi = w_in[l]
    ret_rwkv = wi[:, :7168]
    lora = wi[:, 7168:7296]
    dg = wi[:, 7296:7424]
    qc = wi[:, 7424:7936]
    kvc = wi[:, 7936:8192]
    kr = wi[:, 8192:8256]
    w_u = jnp.concatenate([ret_rwkv, qc, kvc, kr, _rot_half_cols(kr), lora, dg,
                           jnp.zeros((D, U_COLS - 8320), F32)], -1).astype(BF16)

    z64 = jnp.zeros((2, 64, RWKV_W), F32)
    wup = jnp.concatenate([rwkv_wup[l], z64], 1)
    aup = jnp.concatenate([z64, rwkv_aup[l]], 1)

    wq = mla_wuq[l].reshape(MLA_Q_RANK, MLA_HEADS, MLA_NOPE + MLA_ROPE)
    wq_r = wq[..., MLA_NOPE:]
    wq = jnp.concatenate([wq[..., :MLA_NOPE], wq_r, _rot_half_cols(wq_r)], -1)
    wq = wq.reshape(MLA_Q_RANK, MLA_HEADS * MLA_SLOT).astype(BF16)

    wkv = mla_wukv[l].reshape(MLA_KV_RANK, MLA_HEADS, MLA_NOPE + MLA_V)
    wk_top = jnp.concatenate([wkv[..., :MLA_NOPE], jnp.zeros((MLA_KV_RANK, MLA_HEADS, 2 * MLA_ROPE), F32)], -1)
    eye = jnp.eye(2 * MLA_ROPE, dtype=F32)[:, None, :]
    wk_bot = jnp.concatenate([jnp.zeros((2 * MLA_ROPE, MLA_HEADS, MLA_NOPE), F32),
                              jnp.broadcast_to(eye, (2 * MLA_ROPE, MLA_HEADS, 2 * MLA_ROPE))], -1)
    wk = jnp.concatenate([wk_top, wk_bot], 0).reshape(MLA_KV_RANK + 2 * MLA_ROPE, MLA_HEADS * MLA_SLOT).astype(BF16)
    wv = jnp.concatenate([wkv[..., MLA_NOPE:], jnp.zeros((MLA_KV_RANK, MLA_HEADS, MLA_SLOT - MLA_V), F32)], -1)
    wv = wv.reshape(MLA_KV_RANK, MLA_HEADS * MLA_SLOT).astype(BF16)
    return w_u, wup, aup, wq, wk, wv


def _rope_tables(T):
    pos = jnp.arange(T, dtype=F32)

    def cs(half):
        inv = ROPE_THETA ** (-jnp.arange(half, dtype=F32) / half)
        ang = pos[:, None] * inv[None, :]
        return jnp.cos(ang), jnp.sin(ang)

    ret_cos, ret_sin = cs(RET_DK // 2)
    c, s = cs(MLA_ROPE // 2)
    ones = jnp.ones((T, MLA_NOPE), F32)
    zn = jnp.zeros((T, MLA_NOPE), F32)
    zr = jnp.zeros((T, MLA_ROPE), F32)
    ct = jnp.concatenate([ones, c, c, zr], -1)
    st = jnp.concatenate([zn, s, s, zr], -1)
    scale = (MLA_NOPE + MLA_ROPE) ** -0.5 * math.log2(math.e)
    return ret_cos, ret_sin, ct * scale, st * scale, ct, st


def _trunk(x, ada, tables, lw, consts):
    B, T, D = x.shape
    ret_cos, ret_sin, ctq, stq, ctk, stk = tables
    lg_f, lg_b, ones_bd, vt = consts
    tm_prep = min(T, 256)
    for l in range(DEPTH):
        (w_u, w_mg, b_mg, wup, aup, wq, wk, wv, p) = lw[l]
        sh1, sc1, g1, sh2, sc2, g2 = [ada[l][:, None, i * D:(i + 1) * D] for i in range(6)]
        zero_b = jnp.zeros((1, U_COLS), F32)
        u = _inproj(x, sc1, sh1, w_u, zero_b, gate=False, tn=768, name="in_proj")
        gates = _inproj(x, sc1, sh1, w_mg, b_mg, gate=True, tn=1024, name="merge_gates")

        rf = _retention(u, ret_cos, ret_sin, lg_f, rev=False)
        rb = _retention(u, ret_cos, ret_sin, lg_b, rev=True)

        cols = jnp.concatenate([u[:, tm_prep - 1::tm_prep, 4096:7168], u[:, tm_prep - 1::tm_prep, U_LORA:U_LORA + LANES]], -1)
        zrow = jnp.zeros((B, 1, cols.shape[-1]), F32)
        prev_rows = jnp.concatenate([zrow, cols[:, :-1]], 1)[:, :, None, :]
        cols_n = jnp.concatenate([u[:, tm_prep::tm_prep, 4096:7168], u[:, tm_prep::tm_prep, U_LORA:U_LORA + LANES]], -1)
        next_rows = jnp.concatenate([cols_n, zrow], 1)[:, :, None, :]
        outs = _rwkv_prep(u, prev_rows, next_rows, p["mu"], p["w0"], p["a0"], wup, aup, p["gup"], p["kkg"], p["kag"],
                          tm=tm_prep)
        yf, bf = _rwkv_scan(*outs[0:6], p["rk"], rev=False)
        yb, bb = _rwkv_scan(*outs[6:12], p["rk"], rev=True)
        wg = outs[12]

        q, k, v = _mla_prep(u, p["gq"], p["gkv"], wq, wk, wv, ctq, stq, ctk, stk, vt)
        attn = _flash(q, k, v)

        x = _mix_out(rf, rb, u, yf, yb, bf, bb, wg, attn, gates, x, g1, p["gn"], p["lxg"], p["lxb"],
                     p["ln1g"], p["ln1b"], ones_bd, p["wa"], p["wb"], p["wc"], p["wo"])
        x = _mlp(x, sc2, sh2, g2, p["w1"], p["w2"], p["ln2g"], p["ln2b"])
    return x


def kernel(x_prompt, x_sample, c_prompt, c_sample, w_ada, b_ada, w_in, ret_gn, ret_wo, rwkv_mu, rwkv_w0, rwkv_wup, rwkv_a0, rwkv_aup, rwkv_gup, rwkv_kk, rwkv_ka, rwkv_rk, rwkv_lnx_g, rwkv_lnx_b, rwkv_wo, mla_qnorm, mla_wuq, mla_kvnorm, mla_wukv, mla_wo, w_merge, b_merge, w_out, ln_g, ln_b, w_mlp1, w_mlp2):
    D = D_MODEL
    nb_p = c_prompt.shape[0]
    nb_s = c_sample.shape[0]
    c_pad = jnp.concatenate([c_prompt, c_sample, jnp.zeros((8 - nb_p - nb_s, D), F32)], 0)
    ada = _ada_all(c_pad, w_ada, b_ada)

    lw = []
    for l in range(DEPTH):
        w_u, wup, aup, wq, wk, wv = _prep_layer_weights(l, w_in, rwkv_wup, rwkv_aup, mla_wuq, mla_wukv)
        p = dict(
            mu=rwkv_mu[l][:, None, :], w0=rwkv_w0[l][:, None, :], a0=rwkv_a0[l][:, None, :],
            gup=rwkv_gup[l], kkg=rwkv_kk[l][None], kag=rwkv_ka[l][None], rk=rwkv_rk[l][None],
            gq=mla_qnorm[l][None], gkv=mla_kvnorm[l][None], gn=ret_gn[l][None],
            lxg=rwkv_lnx_g[l][None], lxb=rwkv_lnx_b[l][None],
            ln1g=ln_g[l, 0][None], ln1b=ln_b[l, 0][None], ln2g=ln_g[l, 1][None], ln2b=ln_b[l, 1][None],
            wa=ret_wo[l].astype(BF16), wb=rwkv_wo[l].astype(BF16), wc=mla_wo[l].astype(BF16),
            wo=w_out[l].astype(BF16), w1=w_mlp1[l].astype(BF16), w2=w_mlp2[l].astype(BF16),
        )
        lw.append((w_u, w_merge[l].astype(BF16), b_merge[l][None], wup, aup, wq, wk, wv, p))

    log_g = jnp.log1p(-jnp.exp2(-5.0 - jnp.arange(RET_HEADS, dtype=F32)))
    head = jnp.arange(RWKV_W) // RWKV_HEAD
    ones_bd = (head[:, None] == head[None, :]).astype(BF16)
    vt = jnp.tile(jnp.concatenate([jnp.zeros((1, MLA_V), F32), jnp.ones((1, MLA_SLOT - MLA_V), F32)], -1),
                  (1, MLA_HEADS))
    consts = (log_g, log_g[::-1], ones_bd, vt)

    y_p = _trunk(x_prompt, ada[:, :nb_p], _rope_tables(x_prompt.shape[1]), lw, consts)
    y_s = _trunk(x_sample, ada[:, nb_p:nb_p + nb_s], _rope_tables(x_sample.shape[1]), lw, consts)
    return (y_p, y_s)
```

```python
import functools
import math

import jax
import jax.numpy as jnp
from jax import lax
from jax.experimental import pallas as pl
from jax.experimental.pallas import tpu as pltpu

F32 = jnp.float32
BF16 = jnp.bfloat16
HI = lax.Precision.HIGHEST

D_MODEL = 1024
DEPTH = 4
RET_HEADS = 4
RET_DK = 256
RET_CHUNK = 128
RWKV_HEAD = 64
RWKV_HEADS = 16
RWKV_W = 1024
RWKV_PAIRS = RWKV_HEADS // 2
RWKV_CHUNK = 64
MLA_HEADS = 8
MLA_NOPE = 128
MLA_ROPE = 64
MLA_V = 128
MLA_Q_RANK = 512
MLA_KV_RANK = 256
MLA_SLOT = 256
D_FF = 4 * D_MODEL
ROPE_THETA = 10000.0
ALPHA = (2 * DEPTH) ** 0.25
EPS = 1e-5
LANES = 128

U_RET_G = 3
U_RWKV_R = 4
U_QC = 7168
U_KV = 7680
U_LORA = 8064
U_DG = 8192
U_COLS = 8448

VMEM_LIMIT = 56 * 1024 * 1024


def _cparams(sem):
    return pltpu.CompilerParams(dimension_semantics=sem, vmem_limit_bytes=VMEM_LIMIT)


def _dot(a, b, precision=None):
    return jnp.dot(a, b, preferred_element_type=F32, precision=precision)


def _dot_nt(a, b, precision=None):
    return lax.dot_general(a, b, (((1,), (1,)), ((), ())), preferred_element_type=F32, precision=precision)


def _dot_tn(a, b, precision=None):
    return lax.dot_general(a, b, (((0,), (0,)), ((), ())), preferred_element_type=F32, precision=precision)


def _split_bf16(x, n):
    parts = []
    for _ in range(n):
        p = x.astype(BF16)
        parts.append(p)
        x = x - p.astype(F32)
    return parts


def _sigmoid(x):
    return 1.0 / (1.0 + jnp.exp(-x))


def _layer_norm(x, g, b):
    mu = jnp.mean(x, -1, keepdims=True)
    xc = x - mu
    var = jnp.mean(xc * xc, -1, keepdims=True)
    return xc * lax.rsqrt(var + EPS) * g + b


def _ada_kernel(c_ref, w_ref, b_ref, o_ref):
    c = c_ref[...]
    o_ref[0] = _dot(c * _sigmoid(c), w_ref[0], HI) + b_ref[0]


def _ada_all(c_pad, w_ada, b_ada):
    L, D, N = w_ada.shape
    tn = 1536
    return pl.pallas_call(
        _ada_kernel,
        grid=(L, N // tn),
        in_specs=[
            pl.BlockSpec((8, D), lambda l, j: (0, 0)),
            pl.BlockSpec((1, D, tn), lambda l, j: (l, 0, j)),
            pl.BlockSpec((1, 1, tn), lambda l, j: (l, 0, j)),
        ],
        out_specs=pl.BlockSpec((1, 8, tn), lambda l, j: (l, 0, j)),
        out_shape=jax.ShapeDtypeStruct((L, 8, N), F32),
        compiler_params=_cparams(("parallel", "parallel")),
        name="ada",
    )(c_pad, w_ada, b_ada.reshape(L, 1, N))


def _inproj_kernel(x_ref, sc_ref, sh_ref, w_ref, b_ref, o_ref, h_ref, *, gate):
    @pl.when(pl.program_id(2) == 0)
    def _():
        h_ref[...] = (x_ref[0] * (1.0 + sc_ref[0]) + sh_ref[0]).astype(BF16)

    acc = _dot(h_ref[...], w_ref[...])
    if gate:
        acc = _sigmoid(acc + b_ref[...])
    o_ref[0] = acc.astype(o_ref.dtype)


def _inproj(x, sc, sh, w, b, *, gate, tn, name, out_dtype=F32):
    B, T, D = x.shape
    N = w.shape[1]
    tm = min(T, 1024)
    return pl.pallas_call(
        functools.partial(_inproj_kernel, gate=gate),
        grid=(B, T // tm, N // tn),
        in_specs=[
            pl.BlockSpec((1, tm, D), lambda b_, i, j: (b_, i, 0)),
            pl.BlockSpec((1, 1, D), lambda b_, i, j: (b_, 0, 0)),
            pl.BlockSpec((1, 1, D), lambda b_, i, j: (b_, 0, 0)),
            pl.BlockSpec((D, tn), lambda b_, i, j: (0, j)),
            pl.BlockSpec((1, tn), lambda b_, i, j: (0, j)),
        ],
        out_specs=pl.BlockSpec((1, tm, tn), lambda b_, i, j: (b_, i, j)),
        out_shape=jax.ShapeDtypeStruct((B, T, N), out_dtype),
        scratch_shapes=[pltpu.VMEM((tm, D), BF16)],
        compiler_params=_cparams(("parallel", "parallel", "arbitrary")),
        name=name,
    )(x, sc, sh, w, b)


def _ret_kernel(lg_ref, q_ref, k_ref, v_ref, cos_ref, sin_ref, *rest, rev, tb):
    if rev:
        fwd_ref, g_ref, gn_ref, o_ref, state_ref = rest
    else:
        o_ref, state_ref = rest
    C = RET_CHUNK
    half = RET_DK // 2

    @pl.when(pl.program_id(2) == 0)
    def _():
        state_ref[...] = jnp.zeros_like(state_ref)

    lg = lg_ref[pl.program_id(1)]
    row = lax.broadcasted_iota(jnp.int32, (C, C), 0).astype(F32)
    col = lax.broadcasted_iota(jnp.int32, (C, C), 1).astype(F32)
    idx = lax.broadcasted_iota(jnp.int32, (C, 1), 0).astype(F32)
    if rev:
        dist = col - row
        decay_in = jnp.where(dist > 0, jnp.exp(lg * jnp.maximum(dist, 0.0)), 0.0)
        q_scale = jnp.exp(lg * (C - idx))
        k_scale = jnp.exp(lg * idx)
    else:
        dist = row - col
        decay_in = jnp.where(dist >= 0, jnp.exp(lg * jnp.maximum(dist, 0.0)), 0.0)
        q_scale = jnp.exp(lg * (idx + 1.0))
        k_scale = jnp.exp(lg * (C - 1.0 - idx))
    chunk_decay = jnp.exp(lg * C)

    def rot(x, cos, sin):
        x1, x2 = x[:, :half], x[:, half:]
        return jnp.concatenate([x1 * cos - x2 * sin, x1 * sin + x2 * cos], -1)

    n_chunks = tb // C
    state = state_ref[...]
    for cc in range(n_chunks):
        c = n_chunks - 1 - cc if rev else cc
        rows = slice(c * C, (c + 1) * C)
        cos = cos_ref[rows, :]
        sin = sin_ref[rows, :]
        q = rot(q_ref[0, rows, :], cos, sin).astype(BF16)
        k = rot(k_ref[0, rows, :], cos, sin) * (RET_DK ** -0.5)
        v = v_ref[0, rows, :].astype(BF16)
        s = _dot_nt(q, k.astype(BF16)) * decay_in
        o = _dot(s.astype(BF16), v) + _dot(q, state.astype(BF16)) * q_scale
        state = state * chunk_decay + _dot_tn((k * k_scale).astype(BF16), v)
        if rev:
            o = o + fwd_ref[0, rows, :]
            oc = o - jnp.mean(o, -1, keepdims=True)
            on = oc * lax.rsqrt(jnp.mean(oc * oc, -1, keepdims=True) + EPS)
            g = g_ref[0, rows, :]
            o_ref[0, rows, :] = (on * gn_ref[...] * (g * _sigmoid(g))).astype(BF16)
        else:
            o_ref[0, rows, :] = o
    state_ref[...] = state


def _retention(u, cos, sin, lg, *, fwd=None, gn=None):
    rev = fwd is not None
    B, T, _ = u.shape
    tb = min(T, 512)
    nt = T // tb
    tmap = (lambda i: nt - 1 - i) if rev else (lambda i: i)
    head_blk = pl.BlockSpec((1, tb, RET_DK), lambda b_, h, i: (b_, tmap(i), h))
    extra_specs, extra = [], ()
    if rev:
        extra_specs = [head_blk, pl.BlockSpec((1, tb, RET_DK), lambda b_, h, i: (b_, tmap(i), 3 * RET_HEADS + h)),
                       pl.BlockSpec((1, RET_DK), lambda b_, h, i: (0, h))]
        extra = (fwd, u, gn)
    return pl.pallas_call(
        functools.partial(_ret_kernel, rev=rev, tb=tb),
        grid=(B, RET_HEADS, nt),
        in_specs=[
            pl.BlockSpec(memory_space=pltpu.SMEM),
            pl.BlockSpec((1, tb, RET_DK), lambda b_, h, i: (b_, tmap(i), h)),
            pl.BlockSpec((1, tb, RET_DK), lambda b_, h, i: (b_, tmap(i), RET_HEADS + h)),
            pl.BlockSpec((1, tb, RET_DK), lambda b_, h, i: (b_, tmap(i), 2 * RET_HEADS + h)),
            pl.BlockSpec((tb, RET_DK // 2), lambda b_, h, i: (tmap(i), 0)),
            pl.BlockSpec((tb, RET_DK // 2), lambda b_, h, i: (tmap(i), 0)),
        ] + extra_specs,
        out_specs=head_blk,
        out_shape=jax.ShapeDtypeStruct((B, T, RET_HEADS * RET_DK), BF16 if rev else F32),
        scratch_shapes=[pltpu.VMEM((RET_DK, RET_DK), F32)],
        compiler_params=_cparams(("parallel", "parallel", "arbitrary")),
        name="ret_bwd" if rev else "ret_fwd",
    )(lg, u, u, u, cos, sin, *extra)


def _rwkv_prep_kernel(r_ref, k_ref, v_ref, l_ref, dg_ref, pr_ref, pk_ref, pv_ref, pl_ref, nr_ref, nk_ref, nv_ref,
                      nl_ref, mu_ref, w0_ref, a0_ref, wup_ref, aup_ref, gup_ref, kkg_ref, kag_ref, *out_refs, tm):
    g_ref = out_refs[-1]
    W = RWKV_W
    xs = (r_ref[0], k_ref[0], v_ref[0], l_ref[0])
    offs = (0, W, 2 * W, 3 * W)
    row = lax.broadcasted_iota(jnp.int32, (tm, 1), 0)
    lane = lax.broadcasted_iota(jnp.int32, (tm, LANES), 1)
    i = pl.program_id(1)
    has_prev = (i > 0).astype(F32)
    has_next = (i < pl.num_programs(1) - 1).astype(F32)
    prevs = [ref[0, 7:8, :] * has_prev for ref in (pr_ref, pk_ref, pv_ref, pl_ref)]
    nexts = [ref[0, 0:1, :] * has_next for ref in (nr_ref, nk_ref, nv_ref, nl_ref)]
    for d in range(2):
        nbs, edge, shift = (prevs, 0, 1) if d == 0 else (nexts, tm - 1, tm - 1)
        mu = mu_ref[d]
        xd = []
        for x, off, nb in zip(xs, offs, nbs):
            w = x.shape[1]
            shifted = jnp.where(row == edge, nb, pltpu.roll(x, shift, 0))
            xd.append(x + (shifted - x) * mu[:, off:off + w])
        xr, xk, xv, xl = xd
        lhs = jnp.where(lane < 64, jnp.tanh(xl), xl)
        lw = -math.exp(-0.5) * _sigmoid(w0_ref[d] + _dot(lhs, wup_ref[d], HI))
        a = _sigmoid(a0_ref[d] + _dot(lhs, aup_ref[d], HI))
        o = out_refs[6 * d:6 * d + 6]
        o[0][0] = xr
        o[1][0] = lw
        o[2][0] = xk * (1.0 + (a - 1.0) * kag_ref[...])
        o[3][0] = xv
        o[4][0] = xk * kkg_ref[...]
        o[5][0] = a
    g_ref[0] = _dot(_sigmoid(dg_ref[0]), gup_ref[...], HI)


def _rwkv_prep(u, mu, w0, a0, wup, aup, gup, kkg, kag, *, tm):
    B, T, _ = u.shape
    W = RWKV_W
    full = lambda shape: pl.BlockSpec(shape, lambda b_, i: (0,) * len(shape))
    ublk = lambda width, cb: pl.BlockSpec((1, tm, width), lambda b_, i: (b_, i, cb))
    r8 = tm // 8
    before = lambda width, cb: pl.BlockSpec((1, 8, width), lambda b_, i: (b_, jnp.maximum(i * r8 - 1, 0), cb))
    after = lambda width, cb: pl.BlockSpec((1, 8, width), lambda b_, i: (b_, jnp.minimum((i + 1) * r8, T // 8 - 1), cb))
    shifted_cols = [(W, U_RWKV_R), (W, U_RWKV_R + 1), (W, U_RWKV_R + 2), (LANES, U_LORA // LANES)]
    out = jax.ShapeDtypeStruct((B, T, W), F32)
    return pl.pallas_call(
        functools.partial(_rwkv_prep_kernel, tm=tm),
        grid=(B, T // tm),
        in_specs=[
            ublk(W, U_RWKV_R), ublk(W, U_RWKV_R + 1), ublk(W, U_RWKV_R + 2),
            ublk(LANES, U_LORA // LANES), ublk(LANES, U_DG // LANES),
            *[before(w, cb) for w, cb in shifted_cols], *[after(w, cb) for w, cb in shifted_cols],
            full((2, 1, 3 * W + LANES)), full((2, 1, W)), full((2, 1, W)),
            full((2, LANES, W)), full((2, LANES, W)), full((LANES, W)),
            full((1, W)), full((1, W)),
        ],
        out_specs=[pl.BlockSpec((1, tm, W), lambda b_, i: (b_, i, 0))] * 13,
        out_shape=[out] * 13,
        compiler_params=_cparams(("parallel", "parallel")),
        name="rwkv_prep",
    )(*([u] * 13), mu, w0, a0, wup, aup, gup, kkg, kag)


def _rwkv_kernel(r_ref, lw_ref, ke_ref, v_ref, kk_ref, a_ref, rk_ref, *rest, rev, tb):
    if rev:
        (yf_ref, bf_ref, wg_ref, lxg_ref, lxb_ref, o_ref,
         state_ref, p_ref, q_ref, ry_ref, y1_ref, et_ref, bs_ref, ys_ref) = rest
    else:
        y_ref, bon_ref, state_ref, p_ref, q_ref, ry_ref, y1_ref, et_ref, bs_ref = rest
    C = RWKV_CHUNK
    N = 2 * C

    @pl.when(pl.program_id(2) == 0)
    def _():
        state_ref[...] = jnp.zeros_like(state_ref)
        for ref in (p_ref, q_ref, ry_ref, y1_ref, et_ref, bs_ref):
            ref[1] = jnp.zeros(ref.shape[1:], ref.dtype)

    rowi = lax.broadcasted_iota(jnp.int32, (N, N), 0)
    coli = lax.broadcasted_iota(jnp.int32, (N, N), 1)
    same_head = (rowi // C) == (coli // C)
    tr, tc = rowi % C, coli % C
    before = (tc > tr) if rev else (tc < tr)
    strict = (same_head & before).astype(F32)
    incl = (same_head & (before | (tc == tr))).astype(F32)
    eye = (rowi == coli).astype(F32)
    head_mask = same_head.astype(F32)
    head_ones = same_head.astype(BF16)
    level_masks = []
    b = 1
    while b < C:
        level_masks.append(((rowi // (2 * b) == coli // (2 * b)) & (rowi // b != coli // b)).astype(F32))
        b *= 2
    ci = lax.broadcasted_iota(jnp.int32, (C, C), 0)
    cj = lax.broadcasted_iota(jnp.int32, (C, C), 1)
    cum_mat = ((cj >= ci) if rev else (cj <= ci)).astype(BF16)
    rk = rk_ref[...]

    def stack(x):
        return jnp.concatenate([x, x], axis=0) * head_mask

    def head_sum(x):
        return sum(_dot(p, head_ones) for p in _split_bf16(x, 2))

    nc = tb // C
    order = [nc - 1 - j if rev else j for j in range(nc)]
    every = lambda f, *lists: [f(*args) for args in zip(*lists)]

    i = pl.program_id(2)
    slot = i % 2
    prev = 1 - slot

    carry = [state_ref[...]]

    def recur(j):
        c = order[j]
        s = carry[0]
        sb = s.astype(BF16)
        y = _dot_nt(ry_ref[prev, c], sb) + y1_ref[prev, c]
        if rev:
            ys_ref[c * C:(c + 1) * C, :] = y[:C] + y[C:]
        else:
            y_ref[0, c * C:(c + 1) * C, :] = y[:C] + y[C:]
        carry[0] = s * et_ref[prev, c][0:1] + _dot(sb, p_ref[prev, c]) + q_ref[prev, c]

    kk_all = kk_ref[0]
    r_all, ke_all, v_all = r_ref[0], ke_ref[0], v_ref[0]
    kk_all = kk_all * lax.rsqrt(head_sum(kk_all * kk_all) + 1e-12)
    bs_ref[slot] = head_sum(r_all * ke_all * rk) * v_all
    beta_all = kk_all * a_ref[0]
    lw_all = lw_ref[0]
    rows = [slice(c * C, (c + 1) * C) for c in order]
    lw_parts = _split_bf16(lw_all, 3)
    cum = [sum(_dot(cum_mat, p[rw]) for p in lw_parts) for rw in rows]
    recur(0)
    tot = [cm[0:1] if rev else cm[C - 1:C] for cm in cum]
    e_neg = every(lambda cm: jnp.exp(-cm), cum)
    e_end = every(lambda cm, t: jnp.exp(t - cm), cum, tot)
    a_st = every(lambda rw, cm: stack(-kk_all[rw] * jnp.exp(cm - lw_all[rw])).astype(BF16), rows, cum)
    r_stf = every(lambda rw, cm: stack(r_all[rw] * jnp.exp(cm)), rows, cum)
    r_st = every(lambda x: x.astype(BF16), r_stf)
    b_st = every(lambda rw, e: stack(beta_all[rw] * e).astype(BF16), rows, e_neg)
    k_st = every(lambda rw, e: stack(ke_all[rw] * e).astype(BF16), rows, e_neg)
    v_st = every(lambda rw: stack(v_all[rw]).astype(BF16), rows)
    bh_st = every(lambda rw, e: stack(beta_all[rw] * e).astype(BF16), rows, e_end)
    kh_st = every(lambda rw, e: stack(ke_all[rw] * e).astype(BF16), rows, e_end)

    gram = every(lambda a_, r_, b_, k_: _dot_nt(jnp.concatenate([a_, r_], 0), jnp.concatenate([b_, k_], 0)),
                 a_st, r_st, b_st, k_st)
    recur(1)
    m_ab = every(lambda g: g[:N, :N] * strict, gram)
    m_ak = every(lambda g: (g[:N, N:] * strict).astype(BF16), gram)
    n_rb = every(lambda g: (g[N:, :N] * incl).astype(BF16), gram)
    n_rk = every(lambda g: (g[N:, N:] * incl).astype(BF16), gram)

    x = every(lambda m: eye + m * level_masks[0], m_ab)
    for lvl, mask in enumerate(level_masks[1:]):
        xb = every(lambda x_: x_.astype(BF16), x)
        t = every(lambda xb_, m: _dot(xb_, (m * mask).astype(BF16)).astype(BF16), xb, m_ab)
        x = every(lambda x_, t_, xb_: x_ + _dot(t_, xb_), x, t, xb)
        recur(2 + lvl)
    xb = every(lambda x_: x_.astype(BF16), x)

    z = every(lambda m, v_: _dot(m, v_).astype(BF16), m_ak, v_st)
    w12 = every(lambda xb_, a_, z_: _dot(xb_, jnp.concatenate([a_, z_], 1)).astype(BF16), xb, a_st, z)
    recur(nc - 1)
    state_ref[...] = carry[0]
    if rev:
        y = ys_ref[...] + yf_ref[0]
        yc = y - head_sum(y) * (1.0 / RWKV_HEAD)
        var = head_sum(yc * yc) * (1.0 / RWKV_HEAD)
        o = yc * lax.rsqrt(var + EPS) * lxg_ref[...] + lxb_ref[...] + bs_ref[prev] + bf_ref[0]
        o_ref[0] = (o * wg_ref[0]).astype(BF16)
    nw = every(_dot, n_rb, w12)
    y0 = every(_dot, n_rk, v_st)
    bw = every(_dot_tn, w12, bh_st)
    kv = every(_dot_tn, v_st, kh_st)
    for j, c in enumerate(order):
        ry_ref[slot, c] = (r_stf[j] + nw[j][:, :N]).astype(BF16)
        y1_ref[slot, c] = nw[j][:, N:] + y0[j]
        p_ref[slot, c] = bw[j][:N].astype(BF16)
        q_ref[slot, c] = bw[j][N:] + kv[j]
        et_ref[slot, c] = jnp.broadcast_to(jnp.exp(tot[j]), (8, N))
    if not rev:
        bon_ref[0] = bs_ref[prev]


def _rwkv_scan(r, lw, ke, v, kk, a, rk, *, fwd=None):
    rev = fwd is not None
    B, T, W = r.shape
    tb = min(T, 512)
    nt = T // tb
    nc = tb // RWKV_CHUNK
    assert nc == 8, "the recurrence steps are placed between 8 preparation stages"
    tmap = (lambda i: nt - 1 - i) if rev else (lambda i: i)
    blk_in = pl.BlockSpec((1, tb, LANES), lambda b_, p, i: (b_, tmap(jnp.minimum(i, nt - 1)), p))
    blk_out = pl.BlockSpec((1, tb, LANES), lambda b_, p, i: (b_, tmap(jnp.maximum(i - 1, 0)), p))
    vec = pl.BlockSpec((1, LANES), lambda b_, p, i: (0, p))
    scratch = [
        pltpu.VMEM((LANES, LANES), F32),
        pltpu.VMEM((2, nc, LANES, LANES), BF16), pltpu.VMEM((2, nc, LANES, LANES), F32),
        pltpu.VMEM((2, nc, LANES, LANES), BF16), pltpu.VMEM((2, nc, LANES, LANES), F32),
        pltpu.VMEM((2, nc, 8, LANES), F32), pltpu.VMEM((2, tb, LANES), F32),
    ]
    if rev:
        extra_specs, extra = [blk_out] * 3 + [vec, vec], tuple(fwd)
        out_specs, out_shape = blk_out, jax.ShapeDtypeStruct((B, T, W), BF16)
        scratch.append(pltpu.VMEM((tb, LANES), F32))
    else:
        extra_specs, extra = [], ()
        out_specs, out_shape = [blk_out, blk_out], [jax.ShapeDtypeStruct((B, T, W), F32)] * 2
    return pl.pallas_call(
        functools.partial(_rwkv_kernel, rev=rev, tb=tb),
        grid=(B, RWKV_PAIRS, nt + 1),
        in_specs=[blk_in] * 6 + [vec] + extra_specs,
        out_specs=out_specs,
        out_shape=out_shape,
        scratch_shapes=scratch,
        compiler_params=_cparams(("parallel", "parallel", "arbitrary")),
        name="rwkv_bwd" if rev else "rwkv_fwd",
    )(r, lw, ke, v, kk, a, rk, *extra)


def _mla_prep_kernel(qc_ref, kv_ref, gq_ref, gkv_ref, wq_ref, wk_ref, wv_ref, ctq_ref, stq_ref, ctk_ref,
                     stk_ref, vt_ref, q_ref, k_ref, v_ref):
    def rms(x, g):
        return x * lax.rsqrt(jnp.mean(x * x, -1, keepdims=True) + EPS) * g

    qn = rms(qc_ref[0], gq_ref[...]).astype(BF16)
    kvx = kv_ref[0]
    kvn = rms(kvx[:, :MLA_KV_RANK], gkv_ref[...])
    kin = jnp.concatenate([kvn, kvx[:, MLA_KV_RANK:]], -1).astype(BF16)
    q = _dot(qn, wq_ref[...])
    k = _dot(kin, wk_ref[...])
    v_ref[0] = (_dot(kvn.astype(BF16), wv_ref[...]) + vt_ref[...]).astype(BF16)
    ctq, stq, ctk, stk = ctq_ref[...], stq_ref[...], ctk_ref[...], stk_ref[...]
    for h in range(MLA_HEADS):
        sl = slice(h * MLA_SLOT, (h + 1) * MLA_SLOT)
        qh, kh = q[:, sl], k[:, sl]
        q_ref[0, :, sl] = (qh * ctq + pltpu.roll(qh, MLA_SLOT - MLA_ROPE, 1) * stq).astype(BF16)
        k_ref[0, :, sl] = (kh * ctk + pltpu.roll(kh, MLA_SLOT - MLA_ROPE, 1) * stk).astype(BF16)


def _mla_prep(u, gq, gkv, wq, wk, wv, ctq, stq, ctk, stk, vt):
    B, T, _ = u.shape
    tm = min(T, 512)
    kvw = MLA_KV_RANK + 2 * MLA_ROPE
    QW = MLA_HEADS * MLA_SLOT
    full = lambda shape: pl.BlockSpec(shape, lambda b_, i: (0,) * len(shape))
    tab = pl.BlockSpec((tm, MLA_SLOT), lambda b_, i: (i, 0))
    tok = pl.BlockSpec((1, tm, QW), lambda b_, i: (b_, i, 0))
    return pl.pallas_call(
        _mla_prep_kernel,
        grid=(B, T // tm),
        in_specs=[
            pl.BlockSpec((1, tm, MLA_Q_RANK), lambda b_, i: (b_, i, U_QC // MLA_Q_RANK)),
            pl.BlockSpec((1, tm, kvw), lambda b_, i: (b_, i, U_KV // kvw)),
            full((1, MLA_Q_RANK)), full((1, MLA_KV_RANK)),
            full((MLA_Q_RANK, QW)), full((kvw, QW)), full((MLA_KV_RANK, QW)),
            tab, tab, tab, tab, full((1, QW)),
        ],
        out_specs=[tok, tok, tok],
        out_shape=[jax.ShapeDtypeStruct((B, T, QW), BF16)] * 3,
        compiler_params=_cparams(("parallel", "parallel")),
        name="mla_prep",
    )(u, u, gq, gkv, wq, wk, wv, ctq, stq, ctk, stk, vt)


def _flash_kernel(q_ref, k_ref, v_ref, o_ref, *, tkc, n_kv, unroll):
    q = q_ref[0]
    tq = q.shape[0]

    def step(j, carry):
        m, acc = carry
        off = pl.multiple_of(j * tkc, tkc)
        s = _dot_nt(q, k_ref[0, pl.ds(off, tkc), :])
        m_new = jnp.maximum(m, jnp.max(s, -1, keepdims=True))
        p = jnp.exp2(s - m_new)
        acc = jnp.exp2(m - m_new) * acc + _dot(p.astype(BF16), v_ref[0, pl.ds(off, tkc), :])
        return m_new, acc

    init = (jnp.full((tq, 1), -jnp.inf, F32), jnp.zeros((tq, MLA_SLOT), F32))
    _, acc = lax.fori_loop(0, n_kv, step, init, unroll=unroll)
    o_ref[0] = (acc[:, :MLA_V] / acc[:, MLA_V:]).astype(o_ref.dtype)


def _flash(q, k, v):
    B, T, _ = q.shape
    tq = min(T, 512)
    tkc = min(T, 512)
    n_kv = T // tkc
    resident = lambda: pl.BlockSpec((1, T, MLA_SLOT), lambda b_, h, i: (b_, 0, h), pipeline_mode=pl.Buffered(1))
    return pl.pallas_call(
        functools.partial(_flash_kernel, tkc=tkc, n_kv=n_kv, unroll=min(n_kv, 16)),
        grid=(B, MLA_HEADS, T // tq),
        in_specs=[pl.BlockSpec((1, tq, MLA_SLOT), lambda b_, h, i: (b_, i, h)), resident(), resident()],
        out_specs=pl.BlockSpec((1, tq, MLA_V), lambda b_, h, i: (b_, i, h)),
        out_shape=jax.ShapeDtypeStruct((B, T, MLA_HEADS * MLA_V), BF16),
        compiler_params=_cparams(("parallel", "parallel", "arbitrary")),
        name="mla_attn",
    )(q, k, v)


def _mix_out_kernel(a_ref, b_ref, c_ref, gt_ref, x_ref, g1_ref, lng_ref, lnb_ref, wa_ref, wb_ref, wc_ref, wo_ref, o_ref):
    D = D_MODEL
    gt = gt_ref[0].astype(F32)
    merged = (gt[:, :D] * _dot(a_ref[0], wa_ref[...]) + gt[:, D:2 * D] * _dot(b_ref[0], wb_ref[...])
              + gt[:, 2 * D:] * _dot(c_ref[0], wc_ref[...]))
    mix = _dot(merged.astype(BF16), wo_ref[...])
    o_ref[0] = _layer_norm(ALPHA * x_ref[0] + (1.0 + g1_ref[0]) * mix, lng_ref[...], lnb_ref[...])


def _mix_out(a_in, b_in, c_in, gates, x, g1, lng, lnb, wa, wb, wc, wo):
    B, T, D = x.shape
    tm = min(T, 512)
    tok = lambda width: pl.BlockSpec((1, tm, width), lambda b_, i: (b_, i, 0))
    vec = pl.BlockSpec((1, D), lambda b_, i: (0, 0))
    mat = pl.BlockSpec((D, D), lambda b_, i: (0, 0))
    return pl.pallas_call(
        _mix_out_kernel,
        grid=(B, T // tm),
        in_specs=[tok(D), tok(D), tok(D), tok(3 * D), tok(D), pl.BlockSpec((1, 1, D), lambda b_, i: (b_, 0, 0)),
                  vec, vec, mat, mat, mat, mat],
        out_specs=tok(D),
        out_shape=jax.ShapeDtypeStruct((B, T, D), F32),
        compiler_params=_cparams(("parallel", "parallel")),
        name="mix_out",
    )(a_in, b_in, c_in, gates, x, g1, lng, lnb, wa, wb, wc, wo)


def _mlp_kernel(x_ref, sc_ref, sh_ref, g2_ref, w1_ref, w2_ref, lng_ref, lnb_ref, o_ref, h_ref, acc_ref):
    f = pl.program_id(2)

    @pl.when(f == 0)
    def _():
        h_ref[...] = (x_ref[0] * (1.0 + sc_ref[0]) + sh_ref[0]).astype(BF16)
        acc_ref[...] = jnp.zeros_like(acc_ref)

    a = jnp.maximum(_dot(h_ref[...], w1_ref[...]), 0.0)
    acc_ref[...] += _dot((a * a).astype(BF16), w2_ref[...])

    @pl.when(f == pl.num_programs(2) - 1)
    def _():
        o_ref[0] = _layer_norm(ALPHA * x_ref[0] + (1.0 + g2_ref[0]) * acc_ref[...], lng_ref[...], lnb_ref[...])


def _mlp(x, sc, sh, g2, w1, w2, lng, lnb):
    B, T, D = x.shape
    F = w1.shape[1]
    tm = min(T, 1024)
    tf = 1024
    ada = pl.BlockSpec((1, 1, D), lambda b_, i, f: (b_, 0, 0))
    vec = pl.BlockSpec((1, D), lambda b_, i, f: (0, 0))
    return pl.pallas_call(
        _mlp_kernel,
        grid=(B, T // tm, F // tf),
        in_specs=[
            pl.BlockSpec((1, tm, D), lambda b_, i, f: (b_, i, 0)), ada, ada, ada,
            pl.BlockSpec((D, tf), lambda b_, i, f: (0, f)),
            pl.BlockSpec((tf, D), lambda b_, i, f: (f, 0)),
            vec, vec,
        ],
        out_specs=pl.BlockSpec((1, tm, D), lambda b_, i, f: (b_, i, 0)),
        out_shape=jax.ShapeDtypeStruct((B, T, D), F32),
        scratch_shapes=[pltpu.VMEM((tm, D), BF16), pltpu.VMEM((tm, D), F32)],
        compiler_params=_cparams(("parallel", "parallel", "arbitrary")),
        name="mlp",
    )(x, sc, sh, g2, w1, w2, lng, lnb)


def _rot_half_cols(w):
    half = w.shape[-1] // 2
    return jnp.concatenate([-w[..., half:], w[..., :half]], -1)


def _prep_layer_weights(l, w_in, rwkv_wup, rwkv_aup, mla_wuq, mla_wukv):
    D = D_MODEL
    wi = w_in[l]
    ret_rwkv = wi[:, :7168]
    lora = wi[:, 7168:7296]
    dg = wi[:, 7296:7424]
    qc = wi[:, 7424:7936]
    kvc = wi[:, 7936:8192]
    kr = wi[:, 8192:8256]
    w_u = jnp.concatenate([ret_rwkv, qc, kvc, kr, _rot_half_cols(kr), lora, dg,
                           jnp.zeros((D, U_COLS - 8320), F32)], -1).astype(BF16)

    z64 = jnp.zeros((2, 64, RWKV_W), F32)
    wup = jnp.concatenate([rwkv_wup[l], z64], 1)
    aup = jnp.concatenate([z64, rwkv_aup[l]], 1)

    wq = mla_wuq[l].reshape(MLA_Q_RANK, MLA_HEADS, MLA_NOPE + MLA_ROPE)
    wq_r = wq[..., MLA_NOPE:]
    wq = jnp.concatenate([wq[..., :MLA_NOPE], wq_r, _rot_half_cols(wq_r)], -1)
    wq = wq.reshape(MLA_Q_RANK, MLA_HEADS * MLA_SLOT).astype(BF16)

    wkv = mla_wukv[l].reshape(MLA_KV_RANK, MLA_HEADS, MLA_NOPE + MLA_V)
    wk_top = jnp.concatenate([wkv[..., :MLA_NOPE], jnp.zeros((MLA_KV_RANK, MLA_HEADS, 2 * MLA_ROPE), F32)], -1)
    eye = jnp.eye(2 * MLA_ROPE, dtype=F32)[:, None, :]
    wk_bot = jnp.concatenate([jnp.zeros((2 * MLA_ROPE, MLA_HEADS, MLA_NOPE), F32),
                              jnp.broadcast_to(eye, (2 * MLA_ROPE, MLA_HEADS, 2 * MLA_ROPE))], -1)
    wk = jnp.concatenate([wk_top, wk_bot], 0).reshape(MLA_KV_RANK + 2 * MLA_ROPE, MLA_HEADS * MLA_SLOT).astype(BF16)
    wv = jnp.concatenate([wkv[..., MLA_NOPE:], jnp.zeros((MLA_KV_RANK, MLA_HEADS, MLA_SLOT - MLA_V), F32)], -1)
    wv = wv.reshape(MLA_KV_RANK, MLA_HEADS * MLA_SLOT).astype(BF16)
    return w_u, wup, aup, wq, wk, wv


def _rope_tables(T):
    pos = jnp.arange(T, dtype=F32)

    def cs(half):
        inv = ROPE_THETA ** (-jnp.arange(half, dtype=F32) / half)
        ang = pos[:, None] * inv[None, :]
        return jnp.cos(ang), jnp.sin(ang)

    ret_cos, ret_sin = cs(RET_DK // 2)
    c, s = cs(MLA_ROPE // 2)
    ones = jnp.ones((T, MLA_NOPE), F32)
    zn = jnp.zeros((T, MLA_NOPE), F32)
    zr = jnp.zeros((T, MLA_ROPE), F32)
    ct = jnp.concatenate([ones, c, c, zr], -1)
    st = jnp.concatenate([zn, s, s, zr], -1)
    scale = (MLA_NOPE + MLA_ROPE) ** -0.5 * math.log2(math.e)
    return ret_cos, ret_sin, ct * scale, st * scale, ct, st


def _trunk(x, ada, tables, lw, consts):
    B, T, D = x.shape
    ret_cos, ret_sin, ctq, stq, ctk, stk = tables
    lg_f, lg_b, vt = consts
    tm_prep = min(T, 256)
    for l in range(DEPTH):
        (w_u, w_mg, b_mg, wup, aup, wq, wk, wv, p) = lw[l]
        sh1, sc1, g1, sh2, sc2, g2 = [ada[l][:, None, i * D:(i + 1) * D] for i in range(6)]
        zero_b = jnp.zeros((1, U_COLS), F32)
        u = _inproj(x, sc1, sh1, w_u, zero_b, gate=False, tn=768, name="in_proj")
        gates = _inproj(x, sc1, sh1, w_mg, b_mg, gate=True, tn=1024, name="merge_gates", out_dtype=BF16)

        a_in = _retention(u, ret_cos, ret_sin, lg_b, fwd=_retention(u, ret_cos, ret_sin, lg_f), gn=p["gn"])

        outs = _rwkv_prep(u, p["mu"], p["w0"], p["a0"], wup, aup, p["gup"], p["kkg"], p["kag"], tm=tm_prep)
        yf, bf = _rwkv_scan(*outs[0:6], p["rk"])
        b_in = _rwkv_scan(*outs[6:12], p["rk"], fwd=(yf, bf, outs[12], p["lxg"], p["lxb"]))

        q, k, v = _mla_prep(u, p["gq"], p["gkv"], wq, wk, wv, ctq, stq, ctk, stk, vt)
        attn = _flash(q, k, v)

        x = _mix_out(a_in, b_in, attn, gates, x, g1, p["ln1g"], p["ln1b"], p["wa"], p["wb"], p["wc"], p["wo"])
        x = _mlp(x, sc2, sh2, g2, p["w1"], p["w2"], p["ln2g"], p["ln2b"])
    return x


def kernel(x_prompt, x_sample, c_prompt, c_sample, w_ada, b_ada, w_in, ret_gn, ret_wo, rwkv_mu, rwkv_w0, rwkv_wup, rwkv_a0, rwkv_aup, rwkv_gup, rwkv_kk, rwkv_ka, rwkv_rk, rwkv_lnx_g, rwkv_lnx_b, rwkv_wo, mla_qnorm, mla_wuq, mla_kvnorm, mla_wukv, mla_wo, w_merge, b_merge, w_out, ln_g, ln_b, w_mlp1, w_mlp2):
    D = D_MODEL
    nb_p = c_prompt.shape[0]
    nb_s = c_sample.shape[0]
    c_pad = jnp.concatenate([c_prompt, c_sample, jnp.zeros((8 - nb_p - nb_s, D), F32)], 0)
    ada = _ada_all(c_pad, w_ada, b_ada)

    lw = []
    for l in range(DEPTH):
        w_u, wup, aup, wq, wk, wv = _prep_layer_weights(l, w_in, rwkv_wup, rwkv_aup, mla_wuq, mla_wukv)
        p = dict(
            mu=rwkv_mu[l][:, None, :], w0=rwkv_w0[l][:, None, :], a0=rwkv_a0[l][:, None, :],
            gup=rwkv_gup[l], kkg=rwkv_kk[l][None], kag=rwkv_ka[l][None], rk=rwkv_rk[l][None],
            gq=mla_qnorm[l][None], gkv=mla_kvnorm[l][None], gn=ret_gn[l][None],
            lxg=rwkv_lnx_g[l][None], lxb=rwkv_lnx_b[l][None],
            ln1g=ln_g[l, 0][None], ln1b=ln_b[l, 0][None], ln2g=ln_g[l, 1][None], ln2b=ln_b[l, 1][None],
            wa=ret_wo[l].astype(BF16), wb=rwkv_wo[l].astype(BF16), wc=mla_wo[l].astype(BF16),
            wo=w_out[l].astype(BF16), w1=w_mlp1[l].astype(BF16), w2=w_mlp2[l].astype(BF16),
        )
        lw.append((w_u, w_merge[l].astype(BF16), b_merge[l][None], wup, aup, wq, wk, wv, p))

    log_g = jnp.log1p(-jnp.exp2(-5.0 - jnp.arange(RET_HEADS, dtype=F32)))
    vt = jnp.tile(jnp.concatenate([jnp.zeros((1, MLA_V), F32), jnp.ones((1, MLA_SLOT - MLA_V), F32)], -1),
                  (1, MLA_HEADS))
    consts = (log_g, log_g[::-1], vt)

    y_p = _trunk(x_prompt, ada[:, :nb_p], _rope_tables(x_prompt.shape[1]), lw, consts)
    y_s = _trunk(x_sample, ada[:, nb_p:nb_p + nb_s], _rope_tables(x_sample.shape[1]), lw, consts)
    return (y_p, y_s)
```

```python
import functools
import math

import jax
import jax.numpy as jnp
from jax import lax
from jax.experimental import pallas as pl
from jax.experimental.pallas import tpu as pltpu

F32 = jnp.float32
BF16 = jnp.bfloat16
HI = lax.Precision.HIGHEST

D_MODEL = 1024
DEPTH = 4
RET_HEADS = 4
RET_DK = 256
RET_CHUNK = 128
RWKV_HEAD = 64
RWKV_HEADS = 16
RWKV_W = 1024
RWKV_PAIRS = RWKV_HEADS // 2
RWKV_CHUNK = 64
MLA_HEADS = 8
MLA_NOPE = 128
MLA_ROPE = 64
MLA_V = 128
MLA_Q_RANK = 512
MLA_KV_RANK = 256
MLA_SLOT = 256
D_FF = 4 * D_MODEL
ROPE_THETA = 10000.0
ALPHA = (2 * DEPTH) ** 0.25
EPS = 1e-5
LANES = 128

U_RET_G = 3
U_RWKV_R = 4
U_QC = 7168
U_KV = 7680
U_LORA = 8064
U_DG = 8192
U_COLS = 8448

VMEM_LIMIT = 56 * 1024 * 1024


def _cparams(sem):
    return pltpu.CompilerParams(dimension_semantics=sem, vmem_limit_bytes=VMEM_LIMIT)


def _dot(a, b, precision=None):
    return jnp.dot(a, b, preferred_element_type=F32, precision=precision)


def _dot_nt(a, b, precision=None):
    return lax.dot_general(a, b, (((1,), (1,)), ((), ())), preferred_element_type=F32, precision=precision)


def _dot_tn(a, b, precision=None):
    return lax.dot_general(a, b, (((0,), (0,)), ((), ())), preferred_element_type=F32, precision=precision)


def _split_bf16(x, n):
    parts = []
    for _ in range(n):
        p = x.astype(BF16)
        parts.append(p)
        x = x - p.astype(F32)
    return parts


def _hi_lo(w):
    hi = w.astype(BF16)
    return hi, (w - hi.astype(F32)).astype(BF16)


def _dot3(x, w_hi, w_lo):
    x_hi, x_lo = _split_bf16(x, 2)
    return _dot(x_hi, w_hi) + _dot(x_hi, w_lo) + _dot(x_lo, w_hi)


def _sigmoid(x):
    return 1.0 / (1.0 + jnp.exp(-x))


def _layer_norm(x, g, b):
    mu = jnp.mean(x, -1, keepdims=True)
    xc = x - mu
    var = jnp.mean(xc * xc, -1, keepdims=True)
    return xc * lax.rsqrt(var + EPS) * g + b


def _ada_kernel(c_ref, w_ref, b_ref, o_ref):
    c = c_ref[...]
    o_ref[0] = _dot(c * _sigmoid(c), w_ref[0], HI) + b_ref[0]


def _ada_all(c_pad, w_ada, b_ada):
    L, D, N = w_ada.shape
    tn = 1536
    return pl.pallas_call(
        _ada_kernel,
        grid=(L, N // tn),
        in_specs=[
            pl.BlockSpec((8, D), lambda l, j: (0, 0)),
            pl.BlockSpec((1, D, tn), lambda l, j: (l, 0, j)),
            pl.BlockSpec((1, 1, tn), lambda l, j: (l, 0, j)),
        ],
        out_specs=pl.BlockSpec((1, 8, tn), lambda l, j: (l, 0, j)),
        out_shape=jax.ShapeDtypeStruct((L, 8, N), F32),
        compiler_params=_cparams(("parallel", "parallel")),
        name="ada",
    )(c_pad, w_ada, b_ada.reshape(L, 1, N))


def _inproj_kernel(x_ref, sc_ref, sh_ref, w_ref, b_ref, o_ref, h_ref, *, gate):
    @pl.when(pl.program_id(2) == 0)
    def _():
        h_ref[...] = (x_ref[0] * (1.0 + sc_ref[0]) + sh_ref[0]).astype(BF16)

    acc = _dot(h_ref[...], w_ref[...])
    if gate:
        acc = _sigmoid(acc + b_ref[...])
    o_ref[0] = acc.astype(o_ref.dtype)


def _inproj(x, sc, sh, w, b, *, gate, tn, name, out_dtype=F32):
    B, T, D = x.shape
    N = w.shape[1]
    tm = min(T, 1024)
    return pl.pallas_call(
        functools.partial(_inproj_kernel, gate=gate),
        grid=(B, T // tm, N // tn),
        in_specs=[
            pl.BlockSpec((1, tm, D), lambda b_, i, j: (b_, i, 0)),
            pl.BlockSpec((1, 1, D), lambda b_, i, j: (b_, 0, 0)),
            pl.BlockSpec((1, 1, D), lambda b_, i, j: (b_, 0, 0)),
            pl.BlockSpec((D, tn), lambda b_, i, j: (0, j)),
            pl.BlockSpec((1, tn), lambda b_, i, j: (0, j)),
        ],
        out_specs=pl.BlockSpec((1, tm, tn), lambda b_, i, j: (b_, i, j)),
        out_shape=jax.ShapeDtypeStruct((B, T, N), out_dtype),
        scratch_shapes=[pltpu.VMEM((tm, D), BF16)],
        compiler_params=_cparams(("parallel", "parallel", "arbitrary")),
        name=name,
    )(x, sc, sh, w, b)


def _ret_kernel(lg_ref, q_ref, k_ref, v_ref, cos_ref, sin_ref, *rest, rev, tb):
    if rev:
        fwd_ref, g_ref, gn_ref, o_ref, state_ref = rest
    else:
        o_ref, state_ref = rest
    C = RET_CHUNK
    half = RET_DK // 2

    @pl.when(pl.program_id(2) == 0)
    def _():
        state_ref[...] = jnp.zeros_like(state_ref)

    lg = lg_ref[pl.program_id(1)]
    row = lax.broadcasted_iota(jnp.int32, (C, C), 0).astype(F32)
    col = lax.broadcasted_iota(jnp.int32, (C, C), 1).astype(F32)
    idx = lax.broadcasted_iota(jnp.int32, (C, 1), 0).astype(F32)
    if rev:
        dist = col - row
        decay_in = jnp.where(dist > 0, jnp.exp(lg * jnp.maximum(dist, 0.0)), 0.0)
        q_scale = jnp.exp(lg * (C - idx))
        k_scale = jnp.exp(lg * idx)
    else:
        dist = row - col
        decay_in = jnp.where(dist >= 0, jnp.exp(lg * jnp.maximum(dist, 0.0)), 0.0)
        q_scale = jnp.exp(lg * (idx + 1.0))
        k_scale = jnp.exp(lg * (C - 1.0 - idx))
    chunk_decay = jnp.exp(lg * C)

    def rot(x, cos, sin):
        x1, x2 = x[:, :half], x[:, half:]
        return jnp.concatenate([x1 * cos - x2 * sin, x1 * sin + x2 * cos], -1)

    n_chunks = tb // C
    state = state_ref[...]
    for cc in range(n_chunks):
        c = n_chunks - 1 - cc if rev else cc
        rows = slice(c * C, (c + 1) * C)
        cos = cos_ref[rows, :]
        sin = sin_ref[rows, :]
        q = rot(q_ref[0, rows, :], cos, sin).astype(BF16)
        k = rot(k_ref[0, rows, :], cos, sin) * (RET_DK ** -0.5)
        v = v_ref[0, rows, :].astype(BF16)
        s = _dot_nt(q, k.astype(BF16)) * decay_in
        o = _dot(s.astype(BF16), v) + _dot(q, state.astype(BF16)) * q_scale
        state = state * chunk_decay + _dot_tn((k * k_scale).astype(BF16), v)
        if rev:
            o = o + fwd_ref[0, rows, :]
            oc = o - jnp.mean(o, -1, keepdims=True)
            on = oc * lax.rsqrt(jnp.mean(oc * oc, -1, keepdims=True) + EPS)
            g = g_ref[0, rows, :]
            o_ref[0, rows, :] = (on * gn_ref[...] * (g * _sigmoid(g))).astype(BF16)
        else:
            o_ref[0, rows, :] = o
    state_ref[...] = state


def _retention(u, cos, sin, lg, *, fwd=None, gn=None):
    rev = fwd is not None
    B, T, _ = u.shape
    tb = min(T, 1024)
    nt = T // tb
    tmap = (lambda i: nt - 1 - i) if rev else (lambda i: i)
    head_blk = pl.BlockSpec((1, tb, RET_DK), lambda b_, h, i: (b_, tmap(i), h))
    extra_specs, extra = [], ()
    if rev:
        extra_specs = [head_blk, pl.BlockSpec((1, tb, RET_DK), lambda b_, h, i: (b_, tmap(i), 3 * RET_HEADS + h)),
                       pl.BlockSpec((1, RET_DK), lambda b_, h, i: (0, h))]
        extra = (fwd, u, gn)
    return pl.pallas_call(
        functools.partial(_ret_kernel, rev=rev, tb=tb),
        grid=(B, RET_HEADS, nt),
        in_specs=[
            pl.BlockSpec(memory_space=pltpu.SMEM),
            pl.BlockSpec((1, tb, RET_DK), lambda b_, h, i: (b_, tmap(i), h)),
            pl.BlockSpec((1, tb, RET_DK), lambda b_, h, i: (b_, tmap(i), RET_HEADS + h)),
            pl.BlockSpec((1, tb, RET_DK), lambda b_, h, i: (b_, tmap(i), 2 * RET_HEADS + h)),
            pl.BlockSpec((tb, RET_DK // 2), lambda b_, h, i: (tmap(i), 0)),
            pl.BlockSpec((tb, RET_DK // 2), lambda b_, h, i: (tmap(i), 0)),
        ] + extra_specs,
        out_specs=head_blk,
        out_shape=jax.ShapeDtypeStruct((B, T, RET_HEADS * RET_DK), BF16 if rev else F32),
        scratch_shapes=[pltpu.VMEM((RET_DK, RET_DK), F32)],
        compiler_params=_cparams(("parallel", "parallel", "arbitrary")),
        name="ret_bwd" if rev else "ret_fwd",
    )(lg, u, u, u, cos, sin, *extra)


def _rwkv_prep_kernel(r_ref, k_ref, v_ref, l_ref, dg_ref, pr_ref, pk_ref, pv_ref, pl_ref, nr_ref, nk_ref, nv_ref,
                      nl_ref, mu_ref, w0_ref, a0_ref, wl_hi_ref, wl_lo_ref, gup_hi_ref, gup_lo_ref, kkg_ref, kag_ref,
                      *out_refs, tm):
    g_ref = out_refs[-1]
    W = RWKV_W
    xs = (r_ref[0], k_ref[0], v_ref[0], l_ref[0])
    offs = (0, W, 2 * W, 3 * W)
    row = lax.broadcasted_iota(jnp.int32, (tm, 1), 0)
    lane = lax.broadcasted_iota(jnp.int32, (tm, LANES), 1)
    i = pl.program_id(1)
    has_prev = (i > 0).astype(F32)
    has_next = (i < pl.num_programs(1) - 1).astype(F32)
    prevs = [ref[0, 7:8, :] * has_prev for ref in (pr_ref, pk_ref, pv_ref, pl_ref)]
    nexts = [ref[0, 0:1, :] * has_next for ref in (nr_ref, nk_ref, nv_ref, nl_ref)]
    for d in range(2):
        nbs, edge, shift = (prevs, 0, 1) if d == 0 else (nexts, tm - 1, tm - 1)
        mu = mu_ref[d]
        xd = []
        for x, off, nb in zip(xs, offs, nbs):
            w = x.shape[1]
            shifted = jnp.where(row == edge, nb, pltpu.roll(x, shift, 0))
            xd.append(x + (shifted - x) * mu[:, off:off + w])
        xr, xk, xv, xl = xd
        lhs = jnp.where(lane < 64, jnp.tanh(xl), xl)
        logits = _dot3(lhs, wl_hi_ref[d], wl_lo_ref[d])
        lw = -math.exp(-0.5) * _sigmoid(w0_ref[d] + logits[:, :W])
        a = _sigmoid(a0_ref[d] + logits[:, W:])
        o = out_refs[6 * d:6 * d + 6]
        o[0][0] = xr
        o[1][0] = lw
        o[2][0] = xk * (1.0 + (a - 1.0) * kag_ref[...])
        o[3][0] = xv
        o[4][0] = xk * kkg_ref[...]
        o[5][0] = a
    g_ref[0] = _dot3(_sigmoid(dg_ref[0]), gup_hi_ref[...], gup_lo_ref[...])


def _rwkv_prep(u, mu, w0, a0, wl, gup, kkg, kag, *, tm):
    B, T, _ = u.shape
    W = RWKV_W
    full = lambda shape: pl.BlockSpec(shape, lambda b_, i: (0,) * len(shape))
    ublk = lambda width, cb: pl.BlockSpec((1, tm, width), lambda b_, i: (b_, i, cb))
    r8 = tm // 8
    before = lambda width, cb: pl.BlockSpec((1, 8, width), lambda b_, i: (b_, jnp.maximum(i * r8 - 1, 0), cb))
    after = lambda width, cb: pl.BlockSpec((1, 8, width), lambda b_, i: (b_, jnp.minimum((i + 1) * r8, T // 8 - 1), cb))
    shifted_cols = [(W, U_RWKV_R), (W, U_RWKV_R + 1), (W, U_RWKV_R + 2), (LANES, U_LORA // LANES)]
    out = jax.ShapeDtypeStruct((B, T, W), F32)
    return pl.pallas_call(
        functools.partial(_rwkv_prep_kernel, tm=tm),
        grid=(B, T // tm),
        in_specs=[
            ublk(W, U_RWKV_R), ublk(W, U_RWKV_R + 1), ublk(W, U_RWKV_R + 2),
            ublk(LANES, U_LORA // LANES), ublk(LANES, U_DG // LANES),
            *[before(w, cb) for w, cb in shifted_cols], *[after(w, cb) for w, cb in shifted_cols],
            full((2, 1, 3 * W + LANES)), full((2, 1, W)), full((2, 1, W)),
            full((2, LANES, 2 * W)), full((2, LANES, 2 * W)), full((LANES, W)), full((LANES, W)),
            full((1, W)), full((1, W)),
        ],
        out_specs=[pl.BlockSpec((1, tm, W), lambda b_, i: (b_, i, 0))] * 13,
        out_shape=[out] * 13,
        compiler_params=_cparams(("parallel", "parallel")),
        name="rwkv_prep",
    )(*([u] * 13), mu, w0, a0, *wl, *gup, kkg, kag)


def _rwkv_kernel(r_ref, lw_ref, ke_ref, v_ref, kk_ref, a_ref, rk_ref, *rest, rev, tb):
    if rev:
        (yf_ref, bf_ref, wg_ref, lxg_ref, lxb_ref, o_ref,
         state_ref, p_ref, q_ref, ry_ref, y1_ref, et_ref, bs_ref, ys_ref) = rest
    else:
        y_ref, bon_ref, state_ref, p_ref, q_ref, ry_ref, y1_ref, et_ref, bs_ref = rest
    C = RWKV_CHUNK
    N = 2 * C

    @pl.when(pl.program_id(2) == 0)
    def _():
        state_ref[...] = jnp.zeros_like(state_ref)
        for ref in (p_ref, q_ref, ry_ref, y1_ref, et_ref, bs_ref):
            ref[1] = jnp.zeros(ref.shape[1:], ref.dtype)

    rowi = lax.broadcasted_iota(jnp.int32, (N, N), 0)
    coli = lax.broadcasted_iota(jnp.int32, (N, N), 1)
    head_ones = ((rowi // C) == (coli // C)).astype(BF16)
    tr = lax.broadcasted_iota(jnp.int32, (C, N), 0)
    tc = lax.broadcasted_iota(jnp.int32, (C, N), 1) % C
    before = (tc > tr) if rev else (tc < tr)
    strict = before.astype(F32)
    incl = (before | (tc == tr)).astype(F32)
    eye = (tc == tr).astype(F32)
    level_masks = []
    b = 1
    while b < C:
        level_masks.append(((tr // (2 * b) == tc // (2 * b)) & (tr // b != tc // b)).astype(F32))
        b *= 2
    ci = lax.broadcasted_iota(jnp.int32, (C, C), 0)
    cj = lax.broadcasted_iota(jnp.int32, (C, C), 1)
    cum_mat = ((cj >= ci) if rev else (cj <= ci)).astype(BF16)
    rk = rk_ref[...]

    def stack(x):
        return jnp.concatenate([x, x], axis=0) * head_ones

    def head_sum(x):
        return sum(_dot(p, head_ones) for p in _split_bf16(x, 2))

    nc = tb // C
    order = [nc - 1 - j if rev else j for j in range(nc)]
    every = lambda f, *lists: [f(*args) for args in zip(*lists)]

    i = pl.program_id(2)
    slot = i % 2
    prev = 1 - slot

    carry = [state_ref[...]]

    def recur(j):
        c = order[j]
        s = carry[0]
        sb = s.astype(BF16)
        y = _dot_nt(ry_ref[prev, c], stack(sb)) + y1_ref[prev, c]
        if rev:
            ys_ref[c * C:(c + 1) * C, :] = y
        else:
            y_ref[0, c * C:(c + 1) * C, :] = y
        carry[0] = s * et_ref[prev, c][0:1] + _dot(sb, p_ref[prev, c]) + q_ref[prev, c]

    kk_all = kk_ref[0]
    r_all, ke_all, v_all = r_ref[0], ke_ref[0], v_ref[0]
    kk_all = kk_all * lax.rsqrt(head_sum(kk_all * kk_all) + 1e-12)
    bs_ref[slot] = head_sum(r_all * ke_all * rk) * v_all
    beta_all = kk_all * a_ref[0]
    lw_all = lw_ref[0]
    rows = [slice(c * C, (c + 1) * C) for c in order]
    lw_parts = _split_bf16(lw_all, 3)
    cum = [sum(_dot(cum_mat, p[rw]) for p in lw_parts) for rw in rows]
    recur(0)
    tot = [cm[0:1] if rev else cm[C - 1:C] for cm in cum]
    e_neg = every(lambda cm: jnp.exp(-cm), cum)
    e_end = every(lambda cm, t: jnp.exp(t - cm), cum, tot)
    a_p = every(lambda rw, cm: (-kk_all[rw] * jnp.exp(cm - lw_all[rw])).astype(BF16), rows, cum)
    r_pf = every(lambda rw, cm: r_all[rw] * jnp.exp(cm), rows, cum)
    r_p = every(lambda x: x.astype(BF16), r_pf)
    a_st = every(stack, a_p)
    b_st = every(lambda rw, e: stack((beta_all[rw] * e).astype(BF16)), rows, e_neg)
    k_st = every(lambda rw, e: stack((ke_all[rw] * e).astype(BF16)), rows, e_neg)
    v_st = every(lambda rw: stack(v_all[rw].astype(BF16)), rows)
    bh_st = every(lambda rw, e: stack((beta_all[rw] * e).astype(BF16)), rows, e_end)
    kh_st = every(lambda rw, e: stack((ke_all[rw] * e).astype(BF16)), rows, e_end)

    gram = every(lambda a_, r_, b_, k_: _dot_nt(jnp.concatenate([a_, r_], 0), jnp.concatenate([b_, k_], 0)),
                 a_p, r_p, b_st, k_st)
    recur(1)
    m_ab = every(lambda g: g[:C, :N] * strict, gram)
    m_ak = every(lambda g: (g[:C, N:] * strict).astype(BF16), gram)
    n_rb = every(lambda g: (g[C:, :N] * incl).astype(BF16), gram)
    n_rk = every(lambda g: (g[C:, N:] * incl).astype(BF16), gram)

    mab_st = every(lambda m: stack(m.astype(BF16)), m_ab)
    x = every(lambda m: eye + m * level_masks[0], m_ab)
    for lvl, mask in enumerate(level_masks[1:]):
        xb = every(lambda x_: x_.astype(BF16), x)
        t = every(lambda xb_, m: _dot(xb_, m).astype(BF16), xb, mab_st)
        x = every(lambda x_, t_, xb_: x_ + _dot(t_, stack(xb_)) * mask, x, t, xb)
        recur(2 + lvl)
    xb = every(lambda x_: x_.astype(BF16), x)

    z_st = every(lambda m, v_: stack(_dot(m, v_).astype(BF16)), m_ak, v_st)
    w12 = every(lambda xb_, a_, z_: _dot(xb_, jnp.concatenate([a_, z_], 1)).astype(BF16), xb, a_st, z_st)
    w12 = every(lambda w: jnp.concatenate([stack(w[:, :N]), stack(w[:, N:])], 1), w12)
    recur(nc - 1)
    state_ref[...] = carry[0]
    if rev:
        y = ys_ref[...] + yf_ref[0]
        yc = y - head_sum(y) * (1.0 / RWKV_HEAD)
        var = head_sum(yc * yc) * (1.0 / RWKV_HEAD)
        o = yc * lax.rsqrt(var + EPS) * lxg_ref[...] + lxb_ref[...] + bs_ref[prev] + bf_ref[0]
        o_ref[0] = (o * wg_ref[0]).astype(BF16)
    nw = every(_dot, n_rb, w12)
    y0 = every(_dot, n_rk, v_st)
    bw = every(_dot_tn, w12, bh_st)
    kv = every(_dot_tn, v_st, kh_st)
    for j, c in enumerate(order):
        ry_ref[slot, c] = (r_pf[j] + nw[j][:, :N]).astype(BF16)
        y1_ref[slot, c] = nw[j][:, N:] + y0[j]
        p_ref[slot, c] = bw[j][:N].astype(BF16)
        q_st = bw[j][N:] + kv[j]
        q_ref[slot, c] = q_st[:C] + q_st[C:]
        et_ref[slot, c] = jnp.broadcast_to(jnp.exp(tot[j]), (8, N))
    if not rev:
        bon_ref[0] = bs_ref[prev]


def _rwkv_scan(r, lw, ke, v, kk, a, rk, *, fwd=None):
    rev = fwd is not None
    B, T, W = r.shape
    tb = min(T, 512)
    nt = T // tb
    nc = tb // RWKV_CHUNK
    assert nc == 8, "the recurrence steps are placed between 8 preparation stages"
    tmap = (lambda i: nt - 1 - i) if rev else (lambda i: i)
    blk_in = pl.BlockSpec((1, tb, LANES), lambda b_, p, i: (b_, tmap(jnp.minimum(i, nt - 1)), p))
    blk_out = pl.BlockSpec((1, tb, LANES), lambda b_, p, i: (b_, tmap(jnp.maximum(i - 1, 0)), p))
    vec = pl.BlockSpec((1, LANES), lambda b_, p, i: (0, p))
    scratch = [
        pltpu.VMEM((RWKV_CHUNK, LANES), F32),
        pltpu.VMEM((2, nc, LANES, LANES), BF16), pltpu.VMEM((2, nc, RWKV_CHUNK, LANES), F32),
        pltpu.VMEM((2, nc, RWKV_CHUNK, LANES), BF16), pltpu.VMEM((2, nc, RWKV_CHUNK, LANES), F32),
        pltpu.VMEM((2, nc, 8, LANES), F32), pltpu.VMEM((2, tb, LANES), F32),
    ]
    if rev:
        extra_specs, extra = [blk_out] * 3 + [vec, vec], tuple(fwd)
        out_specs, out_shape = blk_out, jax.ShapeDtypeStruct((B, T, W), BF16)
        scratch.append(pltpu.VMEM((tb, LANES), F32))
    else:
        extra_specs, extra = [], ()
        out_specs, out_shape = [blk_out, blk_out], [jax.ShapeDtypeStruct((B, T, W), F32)] * 2
    return pl.pallas_call(
        functools.partial(_rwkv_kernel, rev=rev, tb=tb),
        grid=(B, RWKV_PAIRS, nt + 1),
        in_specs=[blk_in] * 6 + [vec] + extra_specs,
        out_specs=out_specs,
        out_shape=out_shape,
        scratch_shapes=scratch,
        compiler_params=_cparams(("parallel", "parallel", "arbitrary")),
        name="rwkv_bwd" if rev else "rwkv_fwd",
    )(r, lw, ke, v, kk, a, rk, *extra)


def _mla_prep_kernel(qc_ref, kv_ref, gq_ref, gkv_ref, wq_ref, wk_ref, wv_ref, ctq_ref, stq_ref, ctk_ref,
                     stk_ref, vt_ref, q_ref, k_ref, v_ref):
    def rms(x, g):
        return x * lax.rsqrt(jnp.mean(x * x, -1, keepdims=True) + EPS) * g

    qn = rms(qc_ref[0], gq_ref[...]).astype(BF16)
    kvx = kv_ref[0]
    kvn = rms(kvx[:, :MLA_KV_RANK], gkv_ref[...])
    kin = jnp.concatenate([kvn, kvx[:, MLA_KV_RANK:]], -1).astype(BF16)
    q = _dot(qn, wq_ref[...])
    k = _dot(kin, wk_ref[...])
    v_ref[0] = (_dot(kvn.astype(BF16), wv_ref[...]) + vt_ref[...]).astype(BF16)
    ctq, stq, ctk, stk = ctq_ref[...], stq_ref[...], ctk_ref[...], stk_ref[...]
    for h in range(MLA_HEADS):
        sl = slice(h * MLA_SLOT, (h + 1) * MLA_SLOT)
        qh, kh = q[:, sl], k[:, sl]
        q_ref[0, :, sl] = (qh * ctq + pltpu.roll(qh, MLA_SLOT - MLA_ROPE, 1) * stq).astype(BF16)
        k_ref[0, :, sl] = (kh * ctk + pltpu.roll(kh, MLA_SLOT - MLA_ROPE, 1) * stk).astype(BF16)


def _mla_prep(u, gq, gkv, wq, wk, wv, ctq, stq, ctk, stk, vt):
    B, T, _ = u.shape
    tm = min(T, 512)
    kvw = MLA_KV_RANK + 2 * MLA_ROPE
    QW = MLA_HEADS * MLA_SLOT
    full = lambda shape: pl.BlockSpec(shape, lambda b_, i: (0,) * len(shape))
    tab = pl.BlockSpec((tm, MLA_SLOT), lambda b_, i: (i, 0))
    tok = pl.BlockSpec((1, tm, QW), lambda b_, i: (b_, i, 0))
    return pl.pallas_call(
        _mla_prep_kernel,
        grid=(B, T // tm),
        in_specs=[
            pl.BlockSpec((1, tm, MLA_Q_RANK), lambda b_, i: (b_, i, U_QC // MLA_Q_RANK)),
            pl.BlockSpec((1, tm, kvw), lambda b_, i: (b_, i, U_KV // kvw)),
            full((1, MLA_Q_RANK)), full((1, MLA_KV_RANK)),
            full((MLA_Q_RANK, QW)), full((kvw, QW)), full((MLA_KV_RANK, QW)),
            tab, tab, tab, tab, full((1, QW)),
        ],
        out_specs=[tok, tok, tok],
        out_shape=[jax.ShapeDtypeStruct((B, T, QW), BF16)] * 3,
        compiler_params=_cparams(("parallel", "parallel")),
        name="mla_prep",
    )(u, u, gq, gkv, wq, wk, wv, ctq, stq, ctk, stk, vt)


def _flash_kernel(q_ref, k_ref, v_ref, o_ref, *, tkc, n_kv, unroll):
    q = q_ref[0]
    tq = q.shape[0]

    def step(j, carry):
        m, acc = carry
        off = pl.multiple_of(j * tkc, tkc)
        s = _dot_nt(q, k_ref[0, pl.ds(off, tkc), :])
        m_new = jnp.maximum(m, jnp.max(s, -1, keepdims=True))
        p = jnp.exp2(s - m_new)
        acc = jnp.exp2(m - m_new) * acc + _dot(p.astype(BF16), v_ref[0, pl.ds(off, tkc), :])
        return m_new, acc

    init = (jnp.full((tq, 1), -jnp.inf, F32), jnp.zeros((tq, MLA_SLOT), F32))
    _, acc = lax.fori_loop(0, n_kv, step, init, unroll=unroll)
    o_ref[0] = (acc[:, :MLA_V] / acc[:, MLA_V:]).astype(o_ref.dtype)


def _flash(q, k, v):
    B, T, _ = q.shape
    tq = min(T, 1024)
    tkc = min(T, 512)
    n_kv = T // tkc
    resident = lambda: pl.BlockSpec((1, T, MLA_SLOT), lambda b_, h, i: (b_, 0, h), pipeline_mode=pl.Buffered(1))
    return pl.pallas_call(
        functools.partial(_flash_kernel, tkc=tkc, n_kv=n_kv, unroll=min(n_kv, 16)),
        grid=(B, MLA_HEADS, T // tq),
        in_specs=[pl.BlockSpec((1, tq, MLA_SLOT), lambda b_, h, i: (b_, i, h)), resident(), resident()],
        out_specs=pl.BlockSpec((1, tq, MLA_V), lambda b_, h, i: (b_, i, h)),
        out_shape=jax.ShapeDtypeStruct((B, T, MLA_HEADS * MLA_V), BF16),
        compiler_params=_cparams(("parallel", "parallel", "arbitrary")),
        name="mla_attn",
    )(q, k, v)


def _mix_out_kernel(a_ref, b_ref, c_ref, gt_ref, x_ref, g1_ref, lng_ref, lnb_ref, wa_ref, wb_ref, wc_ref, wo_ref, o_ref):
    D = D_MODEL
    gt = gt_ref[0].astype(F32)
    merged = (gt[:, :D] * _dot(a_ref[0], wa_ref[...]) + gt[:, D:2 * D] * _dot(b_ref[0], wb_ref[...])
              + gt[:, 2 * D:] * _dot(c_ref[0], wc_ref[...]))
    mix = _dot(merged.astype(BF16), wo_ref[...])
    o_ref[0] = _layer_norm(ALPHA * x_ref[0] + (1.0 + g1_ref[0]) * mix, lng_ref[...], lnb_ref[...])


def _mix_out(a_in, b_in, c_in, gates, x, g1, lng, lnb, wa, wb, wc, wo):
    B, T, D = x.shape
    tm = min(T, 512)
    tok = lambda width: pl.BlockSpec((1, tm, width), lambda b_, i: (b_, i, 0))
    vec = pl.BlockSpec((1, D), lambda b_, i: (0, 0))
    mat = pl.BlockSpec((D, D), lambda b_, i: (0, 0))
    return pl.pallas_call(
        _mix_out_kernel,
        grid=(B, T // tm),
        in_specs=[tok(D), tok(D), tok(D), tok(3 * D), tok(D), pl.BlockSpec((1, 1, D), lambda b_, i: (b_, 0, 0)),
                  vec, vec, mat, mat, mat, mat],
        out_specs=tok(D),
        out_shape=jax.ShapeDtypeStruct((B, T, D), F32),
        compiler_params=_cparams(("parallel", "parallel")),
        name="mix_out",
    )(a_in, b_in, c_in, gates, x, g1, lng, lnb, wa, wb, wc, wo)


def _mlp_kernel(x_ref, sc_ref, sh_ref, g2_ref, w1_ref, w2_ref, lng_ref, lnb_ref, o_ref, h_ref, acc_ref):
    f = pl.program_id(2)

    @pl.when(f == 0)
    def _():
        h_ref[...] = (x_ref[0] * (1.0 + sc_ref[0]) + sh_ref[0]).astype(BF16)
        acc_ref[...] = jnp.zeros_like(acc_ref)

    a = jnp.maximum(_dot(h_ref[...], w1_ref[...]), 0.0)
    acc_ref[...] += _dot((a * a).astype(BF16), w2_ref[...])

    @pl.when(f == pl.num_programs(2) - 1)
    def _():
        o_ref[0] = _layer_norm(ALPHA * x_ref[0] + (1.0 + g2_ref[0]) * acc_ref[...], lng_ref[...], lnb_ref[...])


def _mlp(x, sc, sh, g2, w1, w2, lng, lnb):
    B, T, D = x.shape
    F = w1.shape[1]
    tm = min(T, 1024)
    tf = 1024
    ada = pl.BlockSpec((1, 1, D), lambda b_, i, f: (b_, 0, 0))
    vec = pl.BlockSpec((1, D), lambda b_, i, f: (0, 0))
    return pl.pallas_call(
        _mlp_kernel,
        grid=(B, T // tm, F // tf),
        in_specs=[
            pl.BlockSpec((1, tm, D), lambda b_, i, f: (b_, i, 0)), ada, ada, ada,
            pl.BlockSpec((D, tf), lambda b_, i, f: (0, f)),
            pl.BlockSpec((tf, D), lambda b_, i, f: (f, 0)),
            vec, vec,
        ],
        out_specs=pl.BlockSpec((1, tm, D), lambda b_, i, f: (b_, i, 0)),
        out_shape=jax.ShapeDtypeStruct((B, T, D), F32),
        scratch_shapes=[pltpu.VMEM((tm, D), BF16), pltpu.VMEM((tm, D), F32)],
        compiler_params=_cparams(("parallel", "parallel", "arbitrary")),
        name="mlp",
    )(x, sc, sh, g2, w1, w2, lng, lnb)


def _rot_half_cols(w):
    half = w.shape[-1] // 2
    return jnp.concatenate([-w[..., half:], w[..., :half]], -1)


def _prep_layer_weights(l, w_in, rwkv_wup, rwkv_aup, mla_wuq, mla_wukv):
    D = D_MODEL
    wi = w_in[l]
    ret_rwkv = wi[:, :7168]
    lora = wi[:, 7168:7296]
    dg = wi[:, 7296:7424]
    qc = wi[:, 7424:7936]
    kvc = wi[:, 7936:8192]
    kr = wi[:, 8192:8256]
    w_u = jnp.concatenate([ret_rwkv, qc, kvc, kr, _rot_half_cols(kr), lora, dg,
                           jnp.zeros((D, U_COLS - 8320), F32)], -1).astype(BF16)

    z64 = jnp.zeros((2, 64, RWKV_W), F32)
    wl = _hi_lo(jnp.concatenate([jnp.concatenate([rwkv_wup[l], z64], 1), jnp.concatenate([z64, rwkv_aup[l]], 1)], -1))

    wq = mla_wuq[l].reshape(MLA_Q_RANK, MLA_HEADS, MLA_NOPE + MLA_ROPE)
    wq_r = wq[..., MLA_NOPE:]
    wq = jnp.concatenate([wq[..., :MLA_NOPE], wq_r, _rot_half_cols(wq_r)], -1)
    wq = wq.reshape(MLA_Q_RANK, MLA_HEADS * MLA_SLOT).astype(BF16)

    wkv = mla_wukv[l].reshape(MLA_KV_RANK, MLA_HEADS, MLA_NOPE + MLA_V)
    wk_top = jnp.concatenate([wkv[..., :MLA_NOPE], jnp.zeros((MLA_KV_RANK, MLA_HEADS, 2 * MLA_ROPE), F32)], -1)
    eye = jnp.eye(2 * MLA_ROPE, dtype=F32)[:, None, :]
    wk_bot = jnp.concatenate([jnp.zeros((2 * MLA_ROPE, MLA_HEADS, MLA_NOPE), F32),
                              jnp.broadcast_to(eye, (2 * MLA_ROPE, MLA_HEADS, 2 * MLA_ROPE))], -1)
    wk = jnp.concatenate([wk_top, wk_bot], 0).reshape(MLA_KV_RANK + 2 * MLA_ROPE, MLA_HEADS * MLA_SLOT).astype(BF16)
    wv = jnp.concatenate([wkv[..., MLA_NOPE:], jnp.zeros((MLA_KV_RANK, MLA_HEADS, MLA_SLOT - MLA_V), F32)], -1)
    wv = wv.reshape(MLA_KV_RANK, MLA_HEADS * MLA_SLOT).astype(BF16)
    return w_u, wl, wq, wk, wv


def _rope_tables(T):
    pos = jnp.arange(T, dtype=F32)

    def cs(half):
        inv = ROPE_THETA ** (-jnp.arange(half, dtype=F32) / half)
        ang = pos[:, None] * inv[None, :]
        return jnp.cos(ang), jnp.sin(ang)

    ret_cos, ret_sin = cs(RET_DK // 2)
    c, s = cs(MLA_ROPE // 2)
    ones = jnp.ones((T, MLA_NOPE), F32)
    zn = jnp.zeros((T, MLA_NOPE), F32)
    zr = jnp.zeros((T, MLA_ROPE), F32)
    ct = jnp.concatenate([ones, c, c, zr], -1)
    st = jnp.concatenate([zn, s, s, zr], -1)
    scale = (MLA_NOPE + MLA_ROPE) ** -0.5 * math.log2(math.e)
    return ret_cos, ret_sin, ct * scale, st * scale, ct, st


def _trunk(x, ada, tables, lw, consts):
    B, T, D = x.shape
    ret_cos, ret_sin, ctq, stq, ctk, stk = tables
    lg_f, lg_b, vt = consts
    tm_prep = min(T, 256)
    for l in range(DEPTH):
        (w_u, w_mg, b_mg, wl, wq, wk, wv, p) = lw[l]
        sh1, sc1, g1, sh2, sc2, g2 = [ada[l][:, None, i * D:(i + 1) * D] for i in range(6)]
        zero_b = jnp.zeros((1, U_COLS), F32)
        u = _inproj(x, sc1, sh1, w_u, zero_b, gate=False, tn=2816, name="in_proj")
        gates = _inproj(x, sc1, sh1, w_mg, b_mg, gate=True, tn=1024, name="merge_gates", out_dtype=BF16)

        a_in = _retention(u, ret_cos, ret_sin, lg_b, fwd=_retention(u, ret_cos, ret_sin, lg_f), gn=p["gn"])

        outs = _rwkv_prep(u, p["mu"], p["w0"], p["a0"], wl, p["gup"], p["kkg"], p["kag"], tm=tm_prep)
        yf, bf = _rwkv_scan(*outs[0:6], p["rk"])
        b_in = _rwkv_scan(*outs[6:12], p["rk"], fwd=(yf, bf, outs[12], p["lxg"], p["lxb"]))

        q, k, v = _mla_prep(u, p["gq"], p["gkv"], wq, wk, wv, ctq, stq, ctk, stk, vt)
        attn = _flash(q, k, v)

        x = _mix_out(a_in, b_in, attn, gates, x, g1, p["ln1g"], p["ln1b"], p["wa"], p["wb"], p["wc"], p["wo"])
        x = _mlp(x, sc2, sh2, g2, p["w1"], p["w2"], p["ln2g"], p["ln2b"])
    return x


def kernel(x_prompt, x_sample, c_prompt, c_sample, w_ada, b_ada, w_in, ret_gn, ret_wo, rwkv_mu, rwkv_w0, rwkv_wup, rwkv_a0, rwkv_aup, rwkv_gup, rwkv_kk, rwkv_ka, rwkv_rk, rwkv_lnx_g, rwkv_lnx_b, rwkv_wo, mla_qnorm, mla_wuq, mla_kvnorm, mla_wukv, mla_wo, w_merge, b_merge, w_out, ln_g, ln_b, w_mlp1, w_mlp2):
    D = D_MODEL
    nb_p = c_prompt.shape[0]
    nb_s = c_sample.shape[0]
    c_pad = jnp.concatenate([c_prompt, c_sample, jnp.zeros((8 - nb_p - nb_s, D), F32)], 0)
    ada = _ada_all(c_pad, w_ada, b_ada)

    lw = []
    for l in range(DEPTH):
        w_u, wl, wq, wk, wv = _prep_layer_weights(l, w_in, rwkv_wup, rwkv_aup, mla_wuq, mla_wukv)
        p = dict(
            mu=rwkv_mu[l][:, None, :], w0=rwkv_w0[l][:, None, :], a0=rwkv_a0[l][:, None, :],
            gup=_hi_lo(rwkv_gup[l]), kkg=rwkv_kk[l][None], kag=rwkv_ka[l][None], rk=rwkv_rk[l][None],
            gq=mla_qnorm[l][None], gkv=mla_kvnorm[l][None], gn=ret_gn[l][None],
            lxg=rwkv_lnx_g[l][None], lxb=rwkv_lnx_b[l][None],
            ln1g=ln_g[l, 0][None], ln1b=ln_b[l, 0][None], ln2g=ln_g[l, 1][None], ln2b=ln_b[l, 1][None],
            wa=ret_wo[l].astype(BF16), wb=rwkv_wo[l].astype(BF16), wc=mla_wo[l].astype(BF16),
            wo=w_out[l].astype(BF16), w1=w_mlp1[l].astype(BF16), w2=w_mlp2[l].astype(BF16),
        )
        lw.append((w_u, w_merge[l].astype(BF16), b_merge[l][None], wl, wq, wk, wv, p))

    log_g = jnp.log1p(-jnp.exp2(-5.0 - jnp.arange(RET_HEADS, dtype=F32)))
    vt = jnp.tile(jnp.concatenate([jnp.zeros((1, MLA_V), F32), jnp.ones((1, MLA_SLOT - MLA_V), F32)], -1),
                  (1, MLA_HEADS))
    consts = (log_g, log_g[::-1], vt)

    y_p = _trunk(x_prompt, ada[:, :nb_p], _rope_tables(x_prompt.shape[1]), lw, consts)
    y_s = _trunk(x_sample, ada[:, nb_p:nb_p + nb_s], _rope_tables(x_sample.shape[1]), lw, consts)
    return (y_p, y_s)
```

```python
import functools
import math

import jax
import jax.numpy as jnp
from jax import lax
from jax.experimental import pallas as pl
from jax.experimental.pallas import tpu as pltpu

F32 = jnp.float32
BF16 = jnp.bfloat16
HI = lax.Precision.HIGHEST

D_MODEL = 1024
DEPTH = 4
RET_HEADS = 4
RET_DK = 256
RET_CHUNK = 128
RWKV_HEAD = 64
RWKV_HEADS = 16
RWKV_W = 1024
RWKV_PAIRS = RWKV_HEADS // 2
RWKV_CHUNK = 64
MLA_HEADS = 8
MLA_NOPE = 128
MLA_ROPE = 64
MLA_V = 128
MLA_Q_RANK = 512
MLA_KV_RANK = 256
MLA_SLOT = 256
D_FF = 4 * D_MODEL
ROPE_THETA = 10000.0
ALPHA = (2 * DEPTH) ** 0.25
EPS = 1e-5
LANES = 128

U_RET_G = 3
U_RWKV_R = 4
U_QC = 7168
U_KV = 7680
U_LORA = 8064
U_DG = 8192
U_COLS = 8448

V7X_VMEM_BYTES = 64 * 1024 * 1024
VMEM_LIMIT = V7X_VMEM_BYTES - 8 * 1024 * 1024
ADA_TN = 1536
PROJ_TM = 1024
INPROJ_TN = U_COLS // 3
GATES_TN = 1024
RET_BLOCK = 1024
RWKV_BLOCK = 512
RWKV_PREP_TM = 256
MLA_PREP_TM = 512
ATTN_TQ = 1024
ATTN_TK = 512
ATTN_UNROLL = 32
MIX_TM = 512
MLP_TF = 1024


def _cparams(sem):
    return pltpu.CompilerParams(dimension_semantics=sem, vmem_limit_bytes=VMEM_LIMIT)


def _dot(a, b, precision=None):
    return jnp.dot(a, b, preferred_element_type=F32, precision=precision)


def _dot_nt(a, b, precision=None):
    return lax.dot_general(a, b, (((1,), (1,)), ((), ())), preferred_element_type=F32, precision=precision)


def _dot_tn(a, b, precision=None):
    return lax.dot_general(a, b, (((0,), (0,)), ((), ())), preferred_element_type=F32, precision=precision)


def _split_bf16(x, n):
    parts = []
    for _ in range(n):
        p = x.astype(BF16)
        parts.append(p)
        x = x - p.astype(F32)
    return parts


def _hi_lo(w):
    hi = w.astype(BF16)
    return hi, (w - hi.astype(F32)).astype(BF16)


def _dot3(x, w_hi, w_lo):
    x_hi, x_lo = _split_bf16(x, 2)
    return _dot(x_hi, w_hi) + _dot(x_hi, w_lo) + _dot(x_lo, w_hi)


def _sigmoid(x):
    return 1.0 / (1.0 + jnp.exp(-x))


def _layer_norm(x, g, b):
    mu = jnp.mean(x, -1, keepdims=True)
    xc = x - mu
    var = jnp.mean(xc * xc, -1, keepdims=True)
    return xc * lax.rsqrt(var + EPS) * g + b


def _ada_kernel(c_ref, w_ref, b_ref, o_ref):
    c = c_ref[...]
    o_ref[0] = _dot(c * _sigmoid(c), w_ref[0], HI) + b_ref[0]


def _ada_all(c_pad, w_ada, b_ada):
    L, D, N = w_ada.shape
    tn = ADA_TN
    return pl.pallas_call(
        _ada_kernel,
        grid=(L, N // tn),
        in_specs=[
            pl.BlockSpec((8, D), lambda l, j: (0, 0)),
            pl.BlockSpec((1, D, tn), lambda l, j: (l, 0, j)),
            pl.BlockSpec((1, 1, tn), lambda l, j: (l, 0, j)),
        ],
        out_specs=pl.BlockSpec((1, 8, tn), lambda l, j: (l, 0, j)),
        out_shape=jax.ShapeDtypeStruct((L, 8, N), F32),
        compiler_params=_cparams(("parallel", "parallel")),
        name="ada",
    )(c_pad, w_ada, b_ada.reshape(L, 1, N))


def _inproj_kernel(x_ref, sc_ref, sh_ref, w_ref, b_ref, o_ref, h_ref, *, gate):
    @pl.when(pl.program_id(2) == 0)
    def _():
        h_ref[...] = (x_ref[0] * (1.0 + sc_ref[0]) + sh_ref[0]).astype(BF16)

    acc = _dot(h_ref[...], w_ref[...])
    if gate:
        acc = _sigmoid(acc + b_ref[...])
    o_ref[0] = acc.astype(o_ref.dtype)


def _inproj(x, sc, sh, w, b, *, gate, tn, name, out_dtype=F32):
    B, T, D = x.shape
    N = w.shape[1]
    tm = min(T, PROJ_TM)
    return pl.pallas_call(
        functools.partial(_inproj_kernel, gate=gate),
        grid=(B, T // tm, N // tn),
        in_specs=[
            pl.BlockSpec((1, tm, D), lambda b_, i, j: (b_, i, 0)),
            pl.BlockSpec((1, 1, D), lambda b_, i, j: (b_, 0, 0)),
            pl.BlockSpec((1, 1, D), lambda b_, i, j: (b_, 0, 0)),
            pl.BlockSpec((D, tn), lambda b_, i, j: (0, j)),
            pl.BlockSpec((1, tn), lambda b_, i, j: (0, j)),
        ],
        out_specs=pl.BlockSpec((1, tm, tn), lambda b_, i, j: (b_, i, j)),
        out_shape=jax.ShapeDtypeStruct((B, T, N), out_dtype),
        scratch_shapes=[pltpu.VMEM((tm, D), BF16)],
        compiler_params=_cparams(("parallel", "parallel", "arbitrary")),
        name=name,
    )(x, sc, sh, w, b)


def _ret_kernel(lg_ref, q_ref, k_ref, v_ref, cos_ref, sin_ref, *rest, rev, tb):
    if rev:
        fwd_ref, g_ref, gn_ref, o_ref, state_ref = rest
    else:
        o_ref, state_ref = rest
    C = RET_CHUNK
    half = RET_DK // 2

    @pl.when(pl.program_id(2) == 0)
    def _():
        state_ref[...] = jnp.zeros_like(state_ref)

    lg = lg_ref[pl.program_id(1)]
    row = lax.broadcasted_iota(jnp.int32, (C, C), 0).astype(F32)
    col = lax.broadcasted_iota(jnp.int32, (C, C), 1).astype(F32)
    idx = lax.broadcasted_iota(jnp.int32, (C, 1), 0).astype(F32)
    if rev:
        dist = col - row
        decay_in = jnp.where(dist > 0, jnp.exp(lg * jnp.maximum(dist, 0.0)), 0.0)
        q_scale = jnp.exp(lg * (C - idx))
        k_scale = jnp.exp(lg * idx)
    else:
        dist = row - col
        decay_in = jnp.where(dist >= 0, jnp.exp(lg * jnp.maximum(dist, 0.0)), 0.0)
        q_scale = jnp.exp(lg * (idx + 1.0))
        k_scale = jnp.exp(lg * (C - 1.0 - idx))
    chunk_decay = jnp.exp(lg * C)

    def rot(x, cos, sin):
        x1, x2 = x[:, :half], x[:, half:]
        return jnp.concatenate([x1 * cos - x2 * sin, x1 * sin + x2 * cos], -1)

    n_chunks = tb // C
    state = state_ref[...]
    for cc in range(n_chunks):
        c = n_chunks - 1 - cc if rev else cc
        rows = slice(c * C, (c + 1) * C)
        cos = cos_ref[rows, :]
        sin = sin_ref[rows, :]
        q = rot(q_ref[0, rows, :], cos, sin).astype(BF16)
        k = rot(k_ref[0, rows, :], cos, sin) * (RET_DK ** -0.5)
        v = v_ref[0, rows, :].astype(BF16)
        s = _dot_nt(q, k.astype(BF16)) * decay_in
        o = _dot(s.astype(BF16), v) + _dot(q, state.astype(BF16)) * q_scale
        state = state * chunk_decay + _dot_tn((k * k_scale).astype(BF16), v)
        if rev:
            o = o + fwd_ref[0, rows, :]
            oc = o - jnp.mean(o, -1, keepdims=True)
            on = oc * lax.rsqrt(jnp.mean(oc * oc, -1, keepdims=True) + EPS)
            g = g_ref[0, rows, :]
            o_ref[0, rows, :] = (on * gn_ref[...] * (g * _sigmoid(g))).astype(BF16)
        else:
            o_ref[0, rows, :] = o
    state_ref[...] = state


def _retention(u, cos, sin, lg, *, fwd=None, gn=None):
    rev = fwd is not None
    B, T, _ = u.shape
    tb = min(T, RET_BLOCK)
    nt = T // tb
    tmap = (lambda i: nt - 1 - i) if rev else (lambda i: i)
    head_blk = pl.BlockSpec((1, tb, RET_DK), lambda b_, h, i: (b_, tmap(i), h))
    extra_specs, extra = [], ()
    if rev:
        extra_specs = [head_blk, pl.BlockSpec((1, tb, RET_DK), lambda b_, h, i: (b_, tmap(i), 3 * RET_HEADS + h)),
                       pl.BlockSpec((1, RET_DK), lambda b_, h, i: (0, h))]
        extra = (fwd, u, gn)
    return pl.pallas_call(
        functools.partial(_ret_kernel, rev=rev, tb=tb),
        grid=(B, RET_HEADS, nt),
        in_specs=[
            pl.BlockSpec(memory_space=pltpu.SMEM),
            pl.BlockSpec((1, tb, RET_DK), lambda b_, h, i: (b_, tmap(i), h)),
            pl.BlockSpec((1, tb, RET_DK), lambda b_, h, i: (b_, tmap(i), RET_HEADS + h)),
            pl.BlockSpec((1, tb, RET_DK), lambda b_, h, i: (b_, tmap(i), 2 * RET_HEADS + h)),
            pl.BlockSpec((tb, RET_DK // 2), lambda b_, h, i: (tmap(i), 0)),
            pl.BlockSpec((tb, RET_DK // 2), lambda b_, h, i: (tmap(i), 0)),
        ] + extra_specs,
        out_specs=head_blk,
        out_shape=jax.ShapeDtypeStruct((B, T, RET_HEADS * RET_DK), BF16 if rev else F32),
        scratch_shapes=[pltpu.VMEM((RET_DK, RET_DK), F32)],
        compiler_params=_cparams(("parallel", "parallel", "arbitrary")),
        name="ret_bwd" if rev else "ret_fwd",
    )(lg, u, u, u, cos, sin, *extra)


def _rwkv_prep_kernel(r_ref, k_ref, v_ref, l_ref, dg_ref, pr_ref, pk_ref, pv_ref, pl_ref, nr_ref, nk_ref, nv_ref,
                      nl_ref, mu_ref, w0_ref, a0_ref, wl_hi_ref, wl_lo_ref, gup_hi_ref, gup_lo_ref, kkg_ref, kag_ref,
                      *out_refs, tm):
    g_ref = out_refs[-1]
    W = RWKV_W
    xs = (r_ref[0], k_ref[0], v_ref[0], l_ref[0])
    offs = (0, W, 2 * W, 3 * W)
    row = lax.broadcasted_iota(jnp.int32, (tm, 1), 0)
    lane = lax.broadcasted_iota(jnp.int32, (tm, LANES), 1)
    i = pl.program_id(1)
    has_prev = (i > 0).astype(F32)
    has_next = (i < pl.num_programs(1) - 1).astype(F32)
    prevs = [ref[0, 7:8, :] * has_prev for ref in (pr_ref, pk_ref, pv_ref, pl_ref)]
    nexts = [ref[0, 0:1, :] * has_next for ref in (nr_ref, nk_ref, nv_ref, nl_ref)]
    for d in range(2):
        nbs, edge, shift = (prevs, 0, 1) if d == 0 else (nexts, tm - 1, tm - 1)
        mu = mu_ref[d]
        xd = []
        for x, off, nb in zip(xs, offs, nbs):
            w = x.shape[1]
            shifted = jnp.where(row == edge, nb, pltpu.roll(x, shift, 0))
            xd.append(x + (shifted - x) * mu[:, off:off + w])
        xr, xk, xv, xl = xd
        lhs = jnp.where(lane < 64, jnp.tanh(xl), xl)
        logits = _dot3(lhs, wl_hi_ref[d], wl_lo_ref[d])
        lw = -math.exp(-0.5) * _sigmoid(w0_ref[d] + logits[:, :W])
        a = _sigmoid(a0_ref[d] + logits[:, W:])
        o = out_refs[6 * d:6 * d + 6]
        o[0][0] = xr
        o[1][0] = lw
        o[2][0] = xk * (1.0 + (a - 1.0) * kag_ref[...])
        o[3][0] = xv
        o[4][0] = xk * kkg_ref[...]
        o[5][0] = a
    g_ref[0] = _dot3(_sigmoid(dg_ref[0]), gup_hi_ref[...], gup_lo_ref[...])


def _rwkv_prep(u, mu, w0, a0, wl, gup, kkg, kag, *, tm):
    B, T, _ = u.shape
    W = RWKV_W
    full = lambda shape: pl.BlockSpec(shape, lambda b_, i: (0,) * len(shape))
    ublk = lambda width, cb: pl.BlockSpec((1, tm, width), lambda b_, i: (b_, i, cb))
    r8 = tm // 8
    before = lambda width, cb: pl.BlockSpec((1, 8, width), lambda b_, i: (b_, jnp.maximum(i * r8 - 1, 0), cb))
    after = lambda width, cb: pl.BlockSpec((1, 8, width), lambda b_, i: (b_, jnp.minimum((i + 1) * r8, T // 8 - 1), cb))
    shifted_cols = [(W, U_RWKV_R), (W, U_RWKV_R + 1), (W, U_RWKV_R + 2), (LANES, U_LORA // LANES)]
    out = jax.ShapeDtypeStruct((B, T, W), F32)
    return pl.pallas_call(
        functools.partial(_rwkv_prep_kernel, tm=tm),
        grid=(B, T // tm),
        in_specs=[
            ublk(W, U_RWKV_R), ublk(W, U_RWKV_R + 1), ublk(W, U_RWKV_R + 2),
            ublk(LANES, U_LORA // LANES), ublk(LANES, U_DG // LANES),
            *[before(w, cb) for w, cb in shifted_cols], *[after(w, cb) for w, cb in shifted_cols],
            full((2, 1, 3 * W + LANES)), full((2, 1, W)), full((2, 1, W)),
            full((2, LANES, 2 * W)), full((2, LANES, 2 * W)), full((LANES, W)), full((LANES, W)),
            full((1, W)), full((1, W)),
        ],
        out_specs=[pl.BlockSpec((1, tm, W), lambda b_, i: (b_, i, 0))] * 13,
        out_shape=[out] * 13,
        compiler_params=_cparams(("parallel", "parallel")),
        name="rwkv_prep",
    )(*([u] * 13), mu, w0, a0, *wl, *gup, kkg, kag)


def _rwkv_kernel(r_ref, lw_ref, ke_ref, v_ref, kk_ref, a_ref, rk_ref, *rest, rev, tb):
    if rev:
        (yf_ref, bf_ref, wg_ref, lxg_ref, lxb_ref, o_ref,
         state_ref, p_ref, q_ref, ry_ref, y1_ref, et_ref, bs_ref, ys_ref) = rest
    else:
        y_ref, bon_ref, state_ref, p_ref, q_ref, ry_ref, y1_ref, et_ref, bs_ref = rest
    C = RWKV_CHUNK
    N = 2 * C

    @pl.when(pl.program_id(2) == 0)
    def _():
        state_ref[...] = jnp.zeros_like(state_ref)
        for ref in (p_ref, q_ref, ry_ref, y1_ref, et_ref, bs_ref):
            ref[1] = jnp.zeros(ref.shape[1:], ref.dtype)

    rowi = lax.broadcasted_iota(jnp.int32, (N, N), 0)
    coli = lax.broadcasted_iota(jnp.int32, (N, N), 1)
    head_ones = ((rowi // C) == (coli // C)).astype(BF16)
    tr = lax.broadcasted_iota(jnp.int32, (C, N), 0)
    tc = lax.broadcasted_iota(jnp.int32, (C, N), 1) % C
    before = (tc > tr) if rev else (tc < tr)
    strict = before.astype(F32)
    incl = (before | (tc == tr)).astype(F32)
    eye = (tc == tr).astype(F32)
    level_masks = []
    b = 1
    while b < C:
        level_masks.append(((tr // (2 * b) == tc // (2 * b)) & (tr // b != tc // b)).astype(F32))
        b *= 2
    ci = lax.broadcasted_iota(jnp.int32, (C, C), 0)
    cj = lax.broadcasted_iota(jnp.int32, (C, C), 1)
    cum_mat = ((cj >= ci) if rev else (cj <= ci)).astype(BF16)
    rk = rk_ref[...]

    def stack(x):
        return jnp.concatenate([x, x], axis=0) * head_ones

    def head_sum(x):
        return sum(_dot(p, head_ones) for p in _split_bf16(x, 2))

    nc = tb // C
    order = [nc - 1 - j if rev else j for j in range(nc)]
    every = lambda f, *lists: [f(*args) for args in zip(*lists)]

    i = pl.program_id(2)
    slot = i % 2
    prev = 1 - slot

    carry = [state_ref[...]]

    def recur(j):
        c = order[j]
        s = carry[0]
        sb = s.astype(BF16)
        y = _dot_nt(ry_ref[prev, c], stack(sb)) + y1_ref[prev, c]
        if rev:
            ys_ref[c * C:(c + 1) * C, :] = y
        else:
            y_ref[0, c * C:(c + 1) * C, :] = y
        carry[0] = s * et_ref[prev, c][0:1] + _dot(sb, p_ref[prev, c]) + q_ref[prev, c]

    kk_all = kk_ref[0]
    r_all, ke_all, v_all = r_ref[0], ke_ref[0], v_ref[0]
    kk_all = kk_all * lax.rsqrt(head_sum(kk_all * kk_all) + 1e-12)
    bs_ref[slot] = head_sum(r_all * ke_all * rk) * v_all
    beta_all = kk_all * a_ref[0]
    lw_all = lw_ref[0]
    rows = [slice(c * C, (c + 1) * C) for c in order]
    lw_parts = _split_bf16(lw_all, 3)
    cum = [sum(_dot(cum_mat, p[rw]) for p in lw_parts) for rw in rows]
    recur(0)
    tot = [cm[0:1] if rev else cm[C - 1:C] for cm in cum]
    e_neg = every(lambda cm: jnp.exp(-cm), cum)
    e_end = every(lambda cm, t: jnp.exp(t - cm), cum, tot)
    a_p = every(lambda rw, cm: (-kk_all[rw] * jnp.exp(cm - lw_all[rw])).astype(BF16), rows, cum)
    r_pf = every(lambda rw, cm: r_all[rw] * jnp.exp(cm), rows, cum)
    r_p = every(lambda x: x.astype(BF16), r_pf)
    a_st = every(stack, a_p)
    b_st = every(lambda rw, e: stack((beta_all[rw] * e).astype(BF16)), rows, e_neg)
    k_st = every(lambda rw, e: stack((ke_all[rw] * e).astype(BF16)), rows, e_neg)
    v_st = every(lambda rw: stack(v_all[rw].astype(BF16)), rows)
    bh_st = every(lambda rw, e: stack((beta_all[rw] * e).astype(BF16)), rows, e_end)
    kh_st = every(lambda rw, e: stack((ke_all[rw] * e).astype(BF16)), rows, e_end)

    gram = every(lambda a_, r_, b_, k_: _dot_nt(jnp.concatenate([a_, r_], 0), jnp.concatenate([b_, k_], 0)),
                 a_p, r_p, b_st, k_st)
    recur(1)
    m_ab = every(lambda g: g[:C, :N] * strict, gram)
    m_ak = every(lambda g: (g[:C, N:] * strict).astype(BF16), gram)
    n_rb = every(lambda g: (g[C:, :N] * incl).astype(BF16), gram)
    n_rk = every(lambda g: (g[C:, N:] * incl).astype(BF16), gram)

    mab_st = every(lambda m: stack(m.astype(BF16)), m_ab)
    x = every(lambda m: eye + m * level_masks[0], m_ab)
    for lvl, mask in enumerate(level_masks[1:]):
        xb = every(lambda x_: x_.astype(BF16), x)
        t = every(lambda xb_, m: _dot(xb_, m).astype(BF16), xb, mab_st)
        x = every(lambda x_, t_, xb_: x_ + _dot(t_, stack(xb_)) * mask, x, t, xb)
        recur(2 + lvl)
    xb = every(lambda x_: x_.astype(BF16), x)

    z_st = every(lambda m, v_: stack(_dot(m, v_).astype(BF16)), m_ak, v_st)
    w12 = every(lambda xb_, a_, z_: _dot(xb_, jnp.concatenate([a_, z_], 1)).astype(BF16), xb, a_st, z_st)
    w12 = every(lambda w: jnp.concatenate([stack(w[:, :N]), stack(w[:, N:])], 1), w12)
    recur(nc - 1)
    state_ref[...] = carry[0]
    if rev:
        y = ys_ref[...] + yf_ref[0]
        yc = y - head_sum(y) * (1.0 / RWKV_HEAD)
        var = head_sum(yc * yc) * (1.0 / RWKV_HEAD)
        o = yc * lax.rsqrt(var + EPS) * lxg_ref[...] + lxb_ref[...] + bs_ref[prev] + bf_ref[0]
        o_ref[0] = (o * wg_ref[0]).astype(BF16)
    nw = every(_dot, n_rb, w12)
    y0 = every(_dot, n_rk, v_st)
    bw = every(_dot_tn, w12, bh_st)
    kv = every(_dot_tn, v_st, kh_st)
    for j, c in enumerate(order):
        ry_ref[slot, c] = (r_pf[j] + nw[j][:, :N]).astype(BF16)
        y1_ref[slot, c] = nw[j][:, N:] + y0[j]
        p_ref[slot, c] = bw[j][:N].astype(BF16)
        q_st = bw[j][N:] + kv[j]
        q_ref[slot, c] = q_st[:C] + q_st[C:]
        et_ref[slot, c] = jnp.broadcast_to(jnp.exp(tot[j]), (8, N))
    if not rev:
        bon_ref[0] = bs_ref[prev]


def _rwkv_scan(r, lw, ke, v, kk, a, rk, *, fwd=None):
    rev = fwd is not None
    B, T, W = r.shape
    tb = min(T, RWKV_BLOCK)
    nt = T // tb
    nc = tb // RWKV_CHUNK
    assert nc == 8, "the recurrence steps are placed between 8 preparation stages"
    tmap = (lambda i: nt - 1 - i) if rev else (lambda i: i)
    blk_in = pl.BlockSpec((1, tb, LANES), lambda b_, p, i: (b_, tmap(jnp.minimum(i, nt - 1)), p))
    blk_out = pl.BlockSpec((1, tb, LANES), lambda b_, p, i: (b_, tmap(jnp.maximum(i - 1, 0)), p))
    vec = pl.BlockSpec((1, LANES), lambda b_, p, i: (0, p))
    scratch = [
        pltpu.VMEM((RWKV_CHUNK, LANES), F32),
        pltpu.VMEM((2, nc, LANES, LANES), BF16), pltpu.VMEM((2, nc, RWKV_CHUNK, LANES), F32),
        pltpu.VMEM((2, nc, RWKV_CHUNK, LANES), BF16), pltpu.VMEM((2, nc, RWKV_CHUNK, LANES), F32),
        pltpu.VMEM((2, nc, 8, LANES), F32), pltpu.VMEM((2, tb, LANES), F32),
    ]
    if rev:
        extra_specs, extra = [blk_out] * 3 + [vec, vec], tuple(fwd)
        out_specs, out_shape = blk_out, jax.ShapeDtypeStruct((B, T, W), BF16)
        scratch.append(pltpu.VMEM((tb, LANES), F32))
    else:
        extra_specs, extra = [], ()
        out_specs, out_shape = [blk_out, blk_out], [jax.ShapeDtypeStruct((B, T, W), F32)] * 2
    return pl.pallas_call(
        functools.partial(_rwkv_kernel, rev=rev, tb=tb),
        grid=(B, RWKV_PAIRS, nt + 1),
        in_specs=[blk_in] * 6 + [vec] + extra_specs,
        out_specs=out_specs,
        out_shape=out_shape,
        scratch_shapes=scratch,
        compiler_params=_cparams(("parallel", "parallel", "arbitrary")),
        name="rwkv_bwd" if rev else "rwkv_fwd",
    )(r, lw, ke, v, kk, a, rk, *extra)


def _mla_prep_kernel(qc_ref, kv_ref, gq_ref, gkv_ref, wq_ref, wk_ref, wv_ref, ctq_ref, stq_ref, ctk_ref,
                     stk_ref, vt_ref, q_ref, k_ref, v_ref):
    def rms(x, g):
        return x * lax.rsqrt(jnp.mean(x * x, -1, keepdims=True) + EPS) * g

    qn = rms(qc_ref[0], gq_ref[...]).astype(BF16)
    kvx = kv_ref[0]
    kvn = rms(kvx[:, :MLA_KV_RANK], gkv_ref[...])
    kin = jnp.concatenate([kvn, kvx[:, MLA_KV_RANK:]], -1).astype(BF16)
    q = _dot(qn, wq_ref[...])
    k = _dot(kin, wk_ref[...])
    v_ref[0] = (_dot(kvn.astype(BF16), wv_ref[...]) + vt_ref[...]).astype(BF16)
    ctq, stq, ctk, stk = ctq_ref[...], stq_ref[...], ctk_ref[...], stk_ref[...]
    for h in range(MLA_HEADS):
        sl = slice(h * MLA_SLOT, (h + 1) * MLA_SLOT)
        qh, kh = q[:, sl], k[:, sl]
        q_ref[0, :, sl] = (qh * ctq + pltpu.roll(qh, MLA_SLOT - MLA_ROPE, 1) * stq).astype(BF16)
        k_ref[0, :, sl] = (kh * ctk + pltpu.roll(kh, MLA_SLOT - MLA_ROPE, 1) * stk).astype(BF16)


def _mla_prep(u, gq, gkv, wq, wk, wv, ctq, stq, ctk, stk, vt):
    B, T, _ = u.shape
    tm = min(T, MLA_PREP_TM)
    kvw = MLA_KV_RANK + 2 * MLA_ROPE
    QW = MLA_HEADS * MLA_SLOT
    full = lambda shape: pl.BlockSpec(shape, lambda b_, i: (0,) * len(shape))
    tab = pl.BlockSpec((tm, MLA_SLOT), lambda b_, i: (i, 0))
    tok = pl.BlockSpec((1, tm, QW), lambda b_, i: (b_, i, 0))
    return pl.pallas_call(
        _mla_prep_kernel,
        grid=(B, T // tm),
        in_specs=[
            pl.BlockSpec((1, tm, MLA_Q_RANK), lambda b_, i: (b_, i, U_QC // MLA_Q_RANK)),
            pl.BlockSpec((1, tm, kvw), lambda b_, i: (b_, i, U_KV // kvw)),
            full((1, MLA_Q_RANK)), full((1, MLA_KV_RANK)),
            full((MLA_Q_RANK, QW)), full((kvw, QW)), full((MLA_KV_RANK, QW)),
            tab, tab, tab, tab, full((1, QW)),
        ],
        out_specs=[tok, tok, tok],
        out_shape=[jax.ShapeDtypeStruct((B, T, QW), BF16)] * 3,
        compiler_params=_cparams(("parallel", "parallel")),
        name="mla_prep",
    )(u, u, gq, gkv, wq, wk, wv, ctq, stq, ctk, stk, vt)


def _flash_kernel(q_ref, k_ref, v_ref, o_ref, *, tkc, n_kv, unroll):
    q = q_ref[0]
    tq = q.shape[0]

    def step(j, carry):
        m, acc = carry
        off = pl.multiple_of(j * tkc, tkc)
        s = _dot_nt(q, k_ref[0, pl.ds(off, tkc), :])
        m_new = jnp.maximum(m, jnp.max(s, -1, keepdims=True))
        p = jnp.exp2(s - m_new)
        acc = jnp.exp2(m - m_new) * acc + _dot(p.astype(BF16), v_ref[0, pl.ds(off, tkc), :])
        return m_new, acc

    init = (jnp.full((tq, 1), -jnp.inf, F32), jnp.zeros((tq, MLA_SLOT), F32))
    _, acc = lax.fori_loop(0, n_kv, step, init, unroll=unroll)
    o_ref[0] = (acc[:, :MLA_V] / acc[:, MLA_V:]).astype(o_ref.dtype)


def _flash(q, k, v):
    B, T, _ = q.shape
    tq = min(T, ATTN_TQ)
    tkc = min(T, ATTN_TK)
    n_kv = T // tkc
    resident = lambda: pl.BlockSpec((1, T, MLA_SLOT), lambda b_, h, i: (b_, 0, h), pipeline_mode=pl.Buffered(1))
    return pl.pallas_call(
        functools.partial(_flash_kernel, tkc=tkc, n_kv=n_kv, unroll=min(n_kv, ATTN_UNROLL)),
        grid=(B, MLA_HEADS, T // tq),
        in_specs=[pl.BlockSpec((1, tq, MLA_SLOT), lambda b_, h, i: (b_, i, h)), resident(), resident()],
        out_specs=pl.BlockSpec((1, tq, MLA_V), lambda b_, h, i: (b_, i, h)),
        out_shape=jax.ShapeDtypeStruct((B, T, MLA_HEADS * MLA_V), BF16),
        compiler_params=_cparams(("parallel", "parallel", "arbitrary")),
        name="mla_attn",
    )(q, k, v)


def _mix_out_kernel(a_ref, b_ref, c_ref, gt_ref, x_ref, g1_ref, lng_ref, lnb_ref, wa_ref, wb_ref, wc_ref, wo_ref, o_ref):
    D = D_MODEL
    gt = gt_ref[0].astype(F32)
    merged = (gt[:, :D] * _dot(a_ref[0], wa_ref[...]) + gt[:, D:2 * D] * _dot(b_ref[0], wb_ref[...])
              + gt[:, 2 * D:] * _dot(c_ref[0], wc_ref[...]))
    mix = _dot(merged.astype(BF16), wo_ref[...])
    o_ref[0] = _layer_norm(ALPHA * x_ref[0] + (1.0 + g1_ref[0]) * mix, lng_ref[...], lnb_ref[...])


def _mix_out(a_in, b_in, c_in, gates, x, g1, lng, lnb, wa, wb, wc, wo):
    B, T, D = x.shape
    tm = min(T, MIX_TM)
    tok = lambda width: pl.BlockSpec((1, tm, width), lambda b_, i: (b_, i, 0))
    vec = pl.BlockSpec((1, D), lambda b_, i: (0, 0))
    mat = pl.BlockSpec((D, D), lambda b_, i: (0, 0))
    return pl.pallas_call(
        _mix_out_kernel,
        grid=(B, T // tm),
        in_specs=[tok(D), tok(D), tok(D), tok(3 * D), tok(D), pl.BlockSpec((1, 1, D), lambda b_, i: (b_, 0, 0)),
                  vec, vec, mat, mat, mat, mat],
        out_specs=tok(D),
        out_shape=jax.ShapeDtypeStruct((B, T, D), F32),
        compiler_params=_cparams(("parallel", "parallel")),
        name="mix_out",
    )(a_in, b_in, c_in, gates, x, g1, lng, lnb, wa, wb, wc, wo)


def _mlp_kernel(x_ref, sc_ref, sh_ref, g2_ref, w1_ref, w2_ref, lng_ref, lnb_ref, o_ref, h_ref, acc_ref):
    f = pl.program_id(2)

    @pl.when(f == 0)
    def _():
        h_ref[...] = (x_ref[0] * (1.0 + sc_ref[0]) + sh_ref[0]).astype(BF16)
        acc_ref[...] = jnp.zeros_like(acc_ref)

    a = jnp.maximum(_dot(h_ref[...], w1_ref[...]), 0.0)
    acc_ref[...] += _dot((a * a).astype(BF16), w2_ref[...])

    @pl.when(f == pl.num_programs(2) - 1)
    def _():
        o_ref[0] = _layer_norm(ALPHA * x_ref[0] + (1.0 + g2_ref[0]) * acc_ref[...], lng_ref[...], lnb_ref[...])


def _mlp(x, sc, sh, g2, w1, w2, lng, lnb):
    B, T, D = x.shape
    F = w1.shape[1]
    tm = min(T, PROJ_TM)
    tf = MLP_TF
    ada = pl.BlockSpec((1, 1, D), lambda b_, i, f: (b_, 0, 0))
    vec = pl.BlockSpec((1, D), lambda b_, i, f: (0, 0))
    return pl.pallas_call(
        _mlp_kernel,
        grid=(B, T // tm, F // tf),
        in_specs=[
            pl.BlockSpec((1, tm, D), lambda b_, i, f: (b_, i, 0)), ada, ada, ada,
            pl.BlockSpec((D, tf), lambda b_, i, f: (0, f)),
            pl.BlockSpec((tf, D), lambda b_, i, f: (f, 0)),
            vec, vec,
        ],
        out_specs=pl.BlockSpec((1, tm, D), lambda b_, i, f: (b_, i, 0)),
        out_shape=jax.ShapeDtypeStruct((B, T, D), F32),
        scratch_shapes=[pltpu.VMEM((tm, D), BF16), pltpu.VMEM((tm, D), F32)],
        compiler_params=_cparams(("parallel", "parallel", "arbitrary")),
        name="mlp",
    )(x, sc, sh, g2, w1, w2, lng, lnb)


def _rot_half_cols(w):
    half = w.shape[-1] // 2
    return jnp.concatenate([-w[..., half:], w[..., :half]], -1)


def _prep_layer_weights(l, w_in, rwkv_wup, rwkv_aup, mla_wuq, mla_wukv):
    D = D_MODEL
    wi = w_in[l]
    ret_rwkv = wi[:, :7168]
    lora = wi[:, 7168:7296]
    dg = wi[:, 7296:7424]
    qc = wi[:, 7424:7936]
    kvc = wi[:, 7936:8192]
    kr = wi[:, 8192:8256]
    w_u = jnp.concatenate([ret_rwkv, qc, kvc, kr, _rot_half_cols(kr), lora, dg,
                           jnp.zeros((D, U_COLS - 8320), F32)], -1).astype(BF16)

    z64 = jnp.zeros((2, 64, RWKV_W), F32)
    wl = _hi_lo(jnp.concatenate([jnp.concatenate([rwkv_wup[l], z64], 1), jnp.concatenate([z64, rwkv_aup[l]], 1)], -1))

    wq = mla_wuq[l].reshape(MLA_Q_RANK, MLA_HEADS, MLA_NOPE + MLA_ROPE)
    wq_r = wq[..., MLA_NOPE:]
    wq = jnp.concatenate([wq[..., :MLA_NOPE], wq_r, _rot_half_cols(wq_r)], -1)
    wq = wq.reshape(MLA_Q_RANK, MLA_HEADS * MLA_SLOT).astype(BF16)

    wkv = mla_wukv[l].reshape(MLA_KV_RANK, MLA_HEADS, MLA_NOPE + MLA_V)
    wk_top = jnp.concatenate([wkv[..., :MLA_NOPE], jnp.zeros((MLA_KV_RANK, MLA_HEADS, 2 * MLA_ROPE), F32)], -1)
    eye = jnp.eye(2 * MLA_ROPE, dtype=F32)[:, None, :]
    wk_bot = jnp.concatenate([jnp.zeros((2 * MLA_ROPE, MLA_HEADS, MLA_NOPE), F32),
                              jnp.broadcast_to(eye, (2 * MLA_ROPE, MLA_HEADS, 2 * MLA_ROPE))], -1)
    wk = jnp.concatenate([wk_top, wk_bot], 0).reshape(MLA_KV_RANK + 2 * MLA_ROPE, MLA_HEADS * MLA_SLOT).astype(BF16)
    wv = jnp.concatenate([wkv[..., MLA_NOPE:], jnp.zeros((MLA_KV_RANK, MLA_HEADS, MLA_SLOT - MLA_V), F32)], -1)
    wv = wv.reshape(MLA_KV_RANK, MLA_HEADS * MLA_SLOT).astype(BF16)
    return w_u, wl, wq, wk, wv


def _rope_tables(T):
    pos = jnp.arange(T, dtype=F32)

    def cs(half):
        inv = ROPE_THETA ** (-jnp.arange(half, dtype=F32) / half)
        ang = pos[:, None] * inv[None, :]
        return jnp.cos(ang), jnp.sin(ang)

    ret_cos, ret_sin = cs(RET_DK // 2)
    c, s = cs(MLA_ROPE // 2)
    ones = jnp.ones((T, MLA_NOPE), F32)
    zn = jnp.zeros((T, MLA_NOPE), F32)
    zr = jnp.zeros((T, MLA_ROPE), F32)
    ct = jnp.concatenate([ones, c, c, zr], -1)
    st = jnp.concatenate([zn, s, s, zr], -1)
    scale = (MLA_NOPE + MLA_ROPE) ** -0.5 * math.log2(math.e)
    return ret_cos, ret_sin, ct * scale, st * scale, ct, st


def _trunk(x, ada, tables, lw, consts):
    B, T, D = x.shape
    ret_cos, ret_sin, ctq, stq, ctk, stk = tables
    lg_f, lg_b, vt = consts
    tm_prep = min(T, RWKV_PREP_TM)
    for l in range(DEPTH):
        (w_u, w_mg, b_mg, wl, wq, wk, wv, p) = lw[l]
        sh1, sc1, g1, sh2, sc2, g2 = [ada[l][:, None, i * D:(i + 1) * D] for i in range(6)]
        zero_b = jnp.zeros((1, U_COLS), F32)
        u = _inproj(x, sc1, sh1, w_u, zero_b, gate=False, tn=INPROJ_TN, name="in_proj")
        gates = _inproj(x, sc1, sh1, w_mg, b_mg, gate=True, tn=GATES_TN, name="merge_gates", out_dtype=BF16)

        a_in = _retention(u, ret_cos, ret_sin, lg_b, fwd=_retention(u, ret_cos, ret_sin, lg_f), gn=p["gn"])

        outs = _rwkv_prep(u, p["mu"], p["w0"], p["a0"], wl, p["gup"], p["kkg"], p["kag"], tm=tm_prep)
        yf, bf = _rwkv_scan(*outs[0:6], p["rk"])
        b_in = _rwkv_scan(*outs[6:12], p["rk"], fwd=(yf, bf, outs[12], p["lxg"], p["lxb"]))

        q, k, v = _mla_prep(u, p["gq"], p["gkv"], wq, wk, wv, ctq, stq, ctk, stk, vt)
        attn = _flash(q, k, v)

        x = _mix_out(a_in, b_in, attn, gates, x, g1, p["ln1g"], p["ln1b"], p["wa"], p["wb"], p["wc"], p["wo"])
        x = _mlp(x, sc2, sh2, g2, p["w1"], p["w2"], p["ln2g"], p["ln2b"])
    return x


def kernel(x_prompt, x_sample, c_prompt, c_sample, w_ada, b_ada, w_in, ret_gn, ret_wo, rwkv_mu, rwkv_w0, rwkv_wup, rwkv_a0, rwkv_aup, rwkv_gup, rwkv_kk, rwkv_ka, rwkv_rk, rwkv_lnx_g, rwkv_lnx_b, rwkv_wo, mla_qnorm, mla_wuq, mla_kvnorm, mla_wukv, mla_wo, w_merge, b_merge, w_out, ln_g, ln_b, w_mlp1, w_mlp2):
    D = D_MODEL
    nb_p = c_prompt.shape[0]
    nb_s = c_sample.shape[0]
    c_pad = jnp.concatenate([c_prompt, c_sample, jnp.zeros((8 - nb_p - nb_s, D), F32)], 0)
    ada = _ada_all(c_pad, w_ada, b_ada)

    lw = []
    for l in range(DEPTH):
        w_u, wl, wq, wk, wv = _prep_layer_weights(l, w_in, rwkv_wup, rwkv_aup, mla_wuq, mla_wukv)
        p = dict(
            mu=rwkv_mu[l][:, None, :], w0=rwkv_w0[l][:, None, :], a0=rwkv_a0[l][:, None, :],
            gup=_hi_lo(rwkv_gup[l]), kkg=rwkv_kk[l][None], kag=rwkv_ka[l][None], rk=rwkv_rk[l][None],
            gq=mla_qnorm[l][None], gkv=mla_kvnorm[l][None], gn=ret_gn[l][None],
            lxg=rwkv_lnx_g[l][None], lxb=rwkv_lnx_b[l][None],
            ln1g=ln_g[l, 0][None], ln1b=ln_b[l, 0][None], ln2g=ln_g[l, 1][None], ln2b=ln_b[l, 1][None],
            wa=ret_wo[l].astype(BF16), wb=rwkv_wo[l].astype(BF16), wc=mla_wo[l].astype(BF16),
            wo=w_out[l].astype(BF16), w1=w_mlp1[l].astype(BF16), w2=w_mlp2[l].astype(BF16),
        )
        lw.append((w_u, w_merge[l].astype(BF16), b_merge[l][None], wl, wq, wk, wv, p))

    log_g = jnp.log1p(-jnp.exp2(-5.0 - jnp.arange(RET_HEADS, dtype=F32)))
    vt = jnp.tile(jnp.concatenate([jnp.zeros((1, MLA_V), F32), jnp.ones((1, MLA_SLOT - MLA_V), F32)], -1),
                  (1, MLA_HEADS))
    consts = (log_g, log_g[::-1], vt)

    y_p = _trunk(x_prompt, ada[:, :nb_p], _rope_tables(x_prompt.shape[1]), lw, consts)
    y_s = _trunk(x_sample, ada[:, nb_p:nb_p + nb_s], _rope_tables(x_sample.shape[1]), lw, consts)
    return (y_p, y_s)
```

```python
import functools
import math

import jax
import jax.numpy as jnp
from jax import lax
from jax.experimental import pallas as pl
from jax.experimental.pallas import tpu as pltpu

F32 = jnp.float32
BF16 = jnp.bfloat16
HI = lax.Precision.HIGHEST

D_MODEL = 1024
DEPTH = 4
RET_HEADS = 4
RET_DK = 256
RET_CHUNK = 128
RWKV_HEAD = 64
RWKV_HEADS = 16
RWKV_W = 1024
RWKV_PAIRS = RWKV_HEADS // 2
RWKV_CHUNK = 64
MLA_HEADS = 8
MLA_NOPE = 128
MLA_ROPE = 64
MLA_V = 128
MLA_Q_RANK = 512
MLA_KV_RANK = 256
MLA_SLOT = 256
D_FF = 4 * D_MODEL
ROPE_THETA = 10000.0
ALPHA = (2 * DEPTH) ** 0.25
EPS = 1e-5
LANES = 128

U_RET_G = 3
U_RWKV_R = 4
U_QC = 7168
U_KV = 7680
U_LORA = 8064
U_DG = 8192
U_COLS = 8448

V7X_VMEM_BYTES = 64 * 1024 * 1024
VMEM_LIMIT = V7X_VMEM_BYTES - 8 * 1024 * 1024
ADA_TN = 1536
PROJ_TM = 1024
INPROJ_TN = U_COLS // 3
GATES_TN = 1024
RET_BLOCK = 2048
RWKV_BLOCK = 1024
RWKV_STAGES = 8
RWKV_PREP_TM = 256
MLA_PREP_TM = 512
ATTN_TQ = 1024
ATTN_TK = 512
ATTN_UNROLL = 32
MIX_TM = 512
MLP_TF = 1024


def _cparams(sem):
    return pltpu.CompilerParams(dimension_semantics=sem, vmem_limit_bytes=VMEM_LIMIT)


def _dot(a, b, precision=None):
    return jnp.dot(a, b, preferred_element_type=F32, precision=precision)


def _dot_nt(a, b, precision=None):
    return lax.dot_general(a, b, (((1,), (1,)), ((), ())), preferred_element_type=F32, precision=precision)


def _dot_tn(a, b, precision=None):
    return lax.dot_general(a, b, (((0,), (0,)), ((), ())), preferred_element_type=F32, precision=precision)


def _split_bf16(x, n):
    parts = []
    for _ in range(n):
        p = x.astype(BF16)
        parts.append(p)
        x = x - p.astype(F32)
    return parts


def _hi_lo(w):
    hi = w.astype(BF16)
    return hi, (w - hi.astype(F32)).astype(BF16)


def _dot3(x, w_hi, w_lo):
    x_hi, x_lo = _split_bf16(x, 2)
    return _dot(x_hi, w_hi) + _dot(x_hi, w_lo) + _dot(x_lo, w_hi)


def _sigmoid(x):
    return 1.0 / (1.0 + jnp.exp(-x))


def _layer_norm(x, g, b):
    mu = jnp.mean(x, -1, keepdims=True)
    xc = x - mu
    var = jnp.mean(xc * xc, -1, keepdims=True)
    return xc * lax.rsqrt(var + EPS) * g + b


def _ada_kernel(c_ref, w_ref, b_ref, o_ref):
    c = c_ref[...]
    o_ref[0] = _dot(c * _sigmoid(c), w_ref[0], HI) + b_ref[0]


def _ada_all(c_pad, w_ada, b_ada):
    L, D, N = w_ada.shape
    tn = ADA_TN
    return pl.pallas_call(
        _ada_kernel,
        grid=(L, N // tn),
        in_specs=[
            pl.BlockSpec((8, D), lambda l, j: (0, 0)),
            pl.BlockSpec((1, D, tn), lambda l, j: (l, 0, j)),
            pl.BlockSpec((1, 1, tn), lambda l, j: (l, 0, j)),
        ],
        out_specs=pl.BlockSpec((1, 8, tn), lambda l, j: (l, 0, j)),
        out_shape=jax.ShapeDtypeStruct((L, 8, N), F32),
        compiler_params=_cparams(("parallel", "parallel")),
        name="ada",
    )(c_pad, w_ada, b_ada.reshape(L, 1, N))


def _inproj_kernel(x_ref, sc_ref, sh_ref, w_ref, b_ref, o_ref, h_ref, *, gate):
    @pl.when(pl.program_id(2) == 0)
    def _():
        h_ref[...] = (x_ref[0] * (1.0 + sc_ref[0]) + sh_ref[0]).astype(BF16)

    acc = _dot(h_ref[...], w_ref[...])
    if gate:
        acc = _sigmoid(acc + b_ref[...])
    o_ref[0] = acc.astype(o_ref.dtype)


def _inproj(x, sc, sh, w, b, *, gate, tn, name, out_dtype=F32):
    B, T, D = x.shape
    N = w.shape[1]
    tm = min(T, PROJ_TM)
    return pl.pallas_call(
        functools.partial(_inproj_kernel, gate=gate),
        grid=(B, T // tm, N // tn),
        in_specs=[
            pl.BlockSpec((1, tm, D), lambda b_, i, j: (b_, i, 0)),
            pl.BlockSpec((1, 1, D), lambda b_, i, j: (b_, 0, 0)),
            pl.BlockSpec((1, 1, D), lambda b_, i, j: (b_, 0, 0)),
            pl.BlockSpec((D, tn), lambda b_, i, j: (0, j)),
            pl.BlockSpec((1, tn), lambda b_, i, j: (0, j)),
        ],
        out_specs=pl.BlockSpec((1, tm, tn), lambda b_, i, j: (b_, i, j)),
        out_shape=jax.ShapeDtypeStruct((B, T, N), out_dtype),
        scratch_shapes=[pltpu.VMEM((tm, D), BF16)],
        compiler_params=_cparams(("parallel", "parallel", "arbitrary")),
        name=name,
    )(x, sc, sh, w, b)


def _ret_kernel(lg_ref, q_ref, k_ref, v_ref, cos_ref, sin_ref, *rest, rev, tb):
    if rev:
        fwd_ref, g_ref, gn_ref, o_ref, state_ref = rest
    else:
        o_ref, state_ref = rest
    C = RET_CHUNK
    half = RET_DK // 2

    @pl.when(pl.program_id(2) == 0)
    def _():
        state_ref[...] = jnp.zeros_like(state_ref)

    lg = lg_ref[pl.program_id(1)]
    row = lax.broadcasted_iota(jnp.int32, (C, C), 0).astype(F32)
    col = lax.broadcasted_iota(jnp.int32, (C, C), 1).astype(F32)
    idx = lax.broadcasted_iota(jnp.int32, (C, 1), 0).astype(F32)
    if rev:
        dist = col - row
        decay_in = jnp.where(dist > 0, jnp.exp(lg * jnp.maximum(dist, 0.0)), 0.0)
        q_scale = jnp.exp(lg * (C - idx))
        k_scale = jnp.exp(lg * idx)
    else:
        dist = row - col
        decay_in = jnp.where(dist >= 0, jnp.exp(lg * jnp.maximum(dist, 0.0)), 0.0)
        q_scale = jnp.exp(lg * (idx + 1.0))
        k_scale = jnp.exp(lg * (C - 1.0 - idx))
    chunk_decay = jnp.exp(lg * C)

    def rot(x, cos, sin):
        x1, x2 = x[:, :half], x[:, half:]
        return jnp.concatenate([x1 * cos - x2 * sin, x1 * sin + x2 * cos], -1)

    n_chunks = tb // C
    state = state_ref[...]
    for cc in range(n_chunks):
        c = n_chunks - 1 - cc if rev else cc
        rows = slice(c * C, (c + 1) * C)
        cos = cos_ref[rows, :]
        sin = sin_ref[rows, :]
        q = rot(q_ref[0, rows, :], cos, sin).astype(BF16)
        k = rot(k_ref[0, rows, :], cos, sin) * (RET_DK ** -0.5)
        v = v_ref[0, rows, :].astype(BF16)
        s = _dot_nt(q, k.astype(BF16)) * decay_in
        o = _dot(s.astype(BF16), v) + _dot(q, state.astype(BF16)) * q_scale
        state = state * chunk_decay + _dot_tn((k * k_scale).astype(BF16), v)
        if rev:
            o = o + fwd_ref[0, rows, :]
            oc = o - jnp.mean(o, -1, keepdims=True)
            on = oc * lax.rsqrt(jnp.mean(oc * oc, -1, keepdims=True) + EPS)
            g = g_ref[0, rows, :]
            o_ref[0, rows, :] = (on * gn_ref[...] * (g * _sigmoid(g))).astype(BF16)
        else:
            o_ref[0, rows, :] = o
    state_ref[...] = state


def _retention(u, cos, sin, lg, *, fwd=None, gn=None):
    rev = fwd is not None
    B, T, _ = u.shape
    tb = min(T, RET_BLOCK)
    nt = T // tb
    tmap = (lambda i: nt - 1 - i) if rev else (lambda i: i)
    head_blk = pl.BlockSpec((1, tb, RET_DK), lambda b_, h, i: (b_, tmap(i), h))
    extra_specs, extra = [], ()
    if rev:
        extra_specs = [head_blk, pl.BlockSpec((1, tb, RET_DK), lambda b_, h, i: (b_, tmap(i), 3 * RET_HEADS + h)),
                       pl.BlockSpec((1, RET_DK), lambda b_, h, i: (0, h))]
        extra = (fwd, u, gn)
    return pl.pallas_call(
        functools.partial(_ret_kernel, rev=rev, tb=tb),
        grid=(B, RET_HEADS, nt),
        in_specs=[
            pl.BlockSpec(memory_space=pltpu.SMEM),
            pl.BlockSpec((1, tb, RET_DK), lambda b_, h, i: (b_, tmap(i), h)),
            pl.BlockSpec((1, tb, RET_DK), lambda b_, h, i: (b_, tmap(i), RET_HEADS + h)),
            pl.BlockSpec((1, tb, RET_DK), lambda b_, h, i: (b_, tmap(i), 2 * RET_HEADS + h)),
            pl.BlockSpec((tb, RET_DK // 2), lambda b_, h, i: (tmap(i), 0)),
            pl.BlockSpec((tb, RET_DK // 2), lambda b_, h, i: (tmap(i), 0)),
        ] + extra_specs,
        out_specs=head_blk,
        out_shape=jax.ShapeDtypeStruct((B, T, RET_HEADS * RET_DK), BF16 if rev else F32),
        scratch_shapes=[pltpu.VMEM((RET_DK, RET_DK), F32)],
        compiler_params=_cparams(("parallel", "parallel", "arbitrary")),
        name="ret_bwd" if rev else "ret_fwd",
    )(lg, u, u, u, cos, sin, *extra)


def _rwkv_prep_kernel(r_ref, k_ref, v_ref, l_ref, dg_ref, pr_ref, pk_ref, pv_ref, pl_ref, nr_ref, nk_ref, nv_ref,
                      nl_ref, mu_ref, w0_ref, a0_ref, wl_hi_ref, wl_lo_ref, gup_hi_ref, gup_lo_ref, kkg_ref, kag_ref,
                      *out_refs, tm):
    g_ref = out_refs[-1]
    W = RWKV_W
    xs = (r_ref[0], k_ref[0], v_ref[0], l_ref[0])
    offs = (0, W, 2 * W, 3 * W)
    row = lax.broadcasted_iota(jnp.int32, (tm, 1), 0)
    lane = lax.broadcasted_iota(jnp.int32, (tm, LANES), 1)
    i = pl.program_id(1)
    has_prev = (i > 0).astype(F32)
    has_next = (i < pl.num_programs(1) - 1).astype(F32)
    prevs = [ref[0, 7:8, :] * has_prev for ref in (pr_ref, pk_ref, pv_ref, pl_ref)]
    nexts = [ref[0, 0:1, :] * has_next for ref in (nr_ref, nk_ref, nv_ref, nl_ref)]
    for d in range(2):
        nbs, edge, shift = (prevs, 0, 1) if d == 0 else (nexts, tm - 1, tm - 1)
        mu = mu_ref[d]
        xd = []
        for x, off, nb in zip(xs, offs, nbs):
            w = x.shape[1]
            shifted = jnp.where(row == edge, nb, pltpu.roll(x, shift, 0))
            xd.append(x + (shifted - x) * mu[:, off:off + w])
        xr, xk, xv, xl = xd
        lhs = jnp.where(lane < 64, jnp.tanh(xl), xl)
        logits = _dot3(lhs, wl_hi_ref[d], wl_lo_ref[d])
        lw = -math.exp(-0.5) * _sigmoid(w0_ref[d] + logits[:, :W])
        a = _sigmoid(a0_ref[d] + logits[:, W:])
        o = out_refs[6 * d:6 * d + 6]
        o[0][0] = xr
        o[1][0] = lw
        o[2][0] = xk * (1.0 + (a - 1.0) * kag_ref[...])
        o[3][0] = xv
        o[4][0] = xk * kkg_ref[...]
        o[5][0] = a
    g_ref[0] = _dot3(_sigmoid(dg_ref[0]), gup_hi_ref[...], gup_lo_ref[...])


def _rwkv_prep(u, mu, w0, a0, wl, gup, kkg, kag, *, tm):
    B, T, _ = u.shape
    W = RWKV_W
    full = lambda shape: pl.BlockSpec(shape, lambda b_, i: (0,) * len(shape))
    ublk = lambda width, cb: pl.BlockSpec((1, tm, width), lambda b_, i: (b_, i, cb))
    r8 = tm // 8
    before = lambda width, cb: pl.BlockSpec((1, 8, width), lambda b_, i: (b_, jnp.maximum(i * r8 - 1, 0), cb))
    after = lambda width, cb: pl.BlockSpec((1, 8, width), lambda b_, i: (b_, jnp.minimum((i + 1) * r8, T // 8 - 1), cb))
    shifted_cols = [(W, U_RWKV_R), (W, U_RWKV_R + 1), (W, U_RWKV_R + 2), (LANES, U_LORA // LANES)]
    out = jax.ShapeDtypeStruct((B, T, W), F32)
    return pl.pallas_call(
        functools.partial(_rwkv_prep_kernel, tm=tm),
        grid=(B, T // tm),
        in_specs=[
            ublk(W, U_RWKV_R), ublk(W, U_RWKV_R + 1), ublk(W, U_RWKV_R + 2),
            ublk(LANES, U_LORA // LANES), ublk(LANES, U_DG // LANES),
            *[before(w, cb) for w, cb in shifted_cols], *[after(w, cb) for w, cb in shifted_cols],
            full((2, 1, 3 * W + LANES)), full((2, 1, W)), full((2, 1, W)),
            full((2, LANES, 2 * W)), full((2, LANES, 2 * W)), full((LANES, W)), full((LANES, W)),
            full((1, W)), full((1, W)),
        ],
        out_specs=[pl.BlockSpec((1, tm, W), lambda b_, i: (b_, i, 0))] * 13,
        out_shape=[out] * 13,
        compiler_params=_cparams(("parallel", "parallel")),
        name="rwkv_prep",
    )(*([u] * 13), mu, w0, a0, *wl, *gup, kkg, kag)


def _rwkv_kernel(r_ref, lw_ref, ke_ref, v_ref, kk_ref, a_ref, rk_ref, *rest, rev, tb):
    if rev:
        (yf_ref, bf_ref, wg_ref, lxg_ref, lxb_ref, o_ref,
         state_ref, p_ref, q_ref, ry_ref, y1_ref, et_ref, bs_ref, ys_ref) = rest
    else:
        y_ref, bon_ref, state_ref, p_ref, q_ref, ry_ref, y1_ref, et_ref, bs_ref = rest
    C = RWKV_CHUNK
    N = 2 * C

    @pl.when(pl.program_id(2) == 0)
    def _():
        state_ref[...] = jnp.zeros_like(state_ref)
        for ref in (p_ref, q_ref, ry_ref, y1_ref, et_ref, bs_ref):
            ref[1] = jnp.zeros(ref.shape[1:], ref.dtype)

    rowi = lax.broadcasted_iota(jnp.int32, (N, N), 0)
    coli = lax.broadcasted_iota(jnp.int32, (N, N), 1)
    head_ones = ((rowi // C) == (coli // C)).astype(BF16)
    tr = lax.broadcasted_iota(jnp.int32, (C, N), 0)
    tc = lax.broadcasted_iota(jnp.int32, (C, N), 1) % C
    before = (tc > tr) if rev else (tc < tr)
    strict = before.astype(F32)
    incl = (before | (tc == tr)).astype(F32)
    eye = (tc == tr).astype(F32)
    level_masks = []
    b = 1
    while b < C:
        level_masks.append(((tr // (2 * b) == tc // (2 * b)) & (tr // b != tc // b)).astype(F32))
        b *= 2
    ci = lax.broadcasted_iota(jnp.int32, (C, C), 0)
    cj = lax.broadcasted_iota(jnp.int32, (C, C), 1)
    cum_mat = ((cj >= ci) if rev else (cj <= ci)).astype(BF16)
    rk = rk_ref[...]

    def stack(x):
        return jnp.concatenate([x, x], axis=0) * head_ones

    def head_sum(x):
        return sum(_dot(p, head_ones) for p in _split_bf16(x, 2))

    nc = tb // C
    order = [nc - 1 - j if rev else j for j in range(nc)]
    every = lambda f, *lists: [f(*args) for args in zip(*lists)]

    i = pl.program_id(2)
    slot = i % 2
    prev = 1 - slot

    carry = [state_ref[...]]

    def recur(stage):
        for j in range(stage * nc // RWKV_STAGES, (stage + 1) * nc // RWKV_STAGES):
            c = order[j]
            s = carry[0]
            sb = s.astype(BF16)
            y = _dot_nt(ry_ref[prev, c], stack(sb)) + y1_ref[prev, c]
            if rev:
                ys_ref[c * C:(c + 1) * C, :] = y
            else:
                y_ref[0, c * C:(c + 1) * C, :] = y
            carry[0] = s * et_ref[prev, c][0:1] + _dot(sb, p_ref[prev, c]) + q_ref[prev, c]

    kk_all = kk_ref[0]
    r_all, ke_all, v_all = r_ref[0], ke_ref[0], v_ref[0]
    kk_all = kk_all * lax.rsqrt(head_sum(kk_all * kk_all) + 1e-12)
    bs_ref[slot] = head_sum(r_all * ke_all * rk) * v_all
    beta_all = kk_all * a_ref[0]
    lw_all = lw_ref[0]
    rows = [slice(c * C, (c + 1) * C) for c in order]
    lw_parts = _split_bf16(lw_all, 3)
    cum = [sum(_dot(cum_mat, p[rw]) for p in lw_parts) for rw in rows]
    recur(0)
    tot = [cm[0:1] if rev else cm[C - 1:C] for cm in cum]
    e_neg = every(lambda cm: jnp.exp(-cm), cum)
    e_end = every(lambda cm, t: jnp.exp(t - cm), cum, tot)
    a_p = every(lambda rw, cm: (-kk_all[rw] * jnp.exp(cm - lw_all[rw])).astype(BF16), rows, cum)
    r_pf = every(lambda rw, cm: r_all[rw] * jnp.exp(cm), rows, cum)
    r_p = every(lambda x: x.astype(BF16), r_pf)
    a_st = every(stack, a_p)
    b_st = every(lambda rw, e: stack((beta_all[rw] * e).astype(BF16)), rows, e_neg)
    k_st = every(lambda rw, e: stack((ke_all[rw] * e).astype(BF16)), rows, e_neg)
    v_st = every(lambda rw: stack(v_all[rw].astype(BF16)), rows)
    bh_st = every(lambda rw, e: stack((beta_all[rw] * e).astype(BF16)), rows, e_end)
    kh_st = every(lambda rw, e: stack((ke_all[rw] * e).astype(BF16)), rows, e_end)

    gram = every(lambda a_, r_, b_, k_: _dot_nt(jnp.concatenate([a_, r_], 0), jnp.concatenate([b_, k_], 0)),
                 a_p, r_p, b_st, k_st)
    recur(1)
    m_ab = every(lambda g: g[:C, :N] * strict, gram)
    m_ak = every(lambda g: (g[:C, N:] * strict).astype(BF16), gram)
    n_rb = every(lambda g: (g[C:, :N] * incl).astype(BF16), gram)
    n_rk = every(lambda g: (g[C:, N:] * incl).astype(BF16), gram)

    mab_st = every(lambda m: stack(m.astype(BF16)), m_ab)
    x = every(lambda m: eye + m * level_masks[0], m_ab)
    for lvl, mask in enumerate(level_masks[1:]):
        xb = every(lambda x_: x_.astype(BF16), x)
        t = every(lambda xb_, m: _dot(xb_, m).astype(BF16), xb, mab_st)
        x = every(lambda x_, t_, xb_: x_ + _dot(t_, stack(xb_)) * mask, x, t, xb)
        recur(2 + lvl)
    xb = every(lambda x_: x_.astype(BF16), x)

    z_st = every(lambda m, v_: stack(_dot(m, v_).astype(BF16)), m_ak, v_st)
    w12 = every(lambda xb_, a_, z_: _dot(xb_, jnp.concatenate([a_, z_], 1)).astype(BF16), xb, a_st, z_st)
    w12 = every(lambda w: jnp.concatenate([stack(w[:, :N]), stack(w[:, N:])], 1), w12)
    recur(RWKV_STAGES - 1)
    state_ref[...] = carry[0]
    if rev:
        y = ys_ref[...] + yf_ref[0]
        yc = y - head_sum(y) * (1.0 / RWKV_HEAD)
        var = head_sum(yc * yc) * (1.0 / RWKV_HEAD)
        o = yc * lax.rsqrt(var + EPS) * lxg_ref[...] + lxb_ref[...] + bs_ref[prev] + bf_ref[0]
        o_ref[0] = (o * wg_ref[0]).astype(BF16)
    nw = every(_dot, n_rb, w12)
    y0 = every(_dot, n_rk, v_st)
    bw = every(_dot_tn, w12, bh_st)
    kv = every(_dot_tn, v_st, kh_st)
    for j, c in enumerate(order):
        ry_ref[slot, c] = (r_pf[j] + nw[j][:, :N]).astype(BF16)
        y1_ref[slot, c] = nw[j][:, N:] + y0[j]
        p_ref[slot, c] = bw[j][:N].astype(BF16)
        q_st = bw[j][N:] + kv[j]
        q_ref[slot, c] = q_st[:C] + q_st[C:]
        et_ref[slot, c] = jnp.broadcast_to(jnp.exp(tot[j]), (8, N))
    if not rev:
        bon_ref[0] = bs_ref[prev]


def _rwkv_scan(r, lw, ke, v, kk, a, rk, *, fwd=None):
    rev = fwd is not None
    B, T, W = r.shape
    tb = min(T, RWKV_BLOCK)
    nt = T // tb
    nc = tb // RWKV_CHUNK
    assert nc % RWKV_STAGES == 0, "the recurrence steps are spread evenly over the preparation stages"
    tmap = (lambda i: nt - 1 - i) if rev else (lambda i: i)
    blk_in = pl.BlockSpec((1, tb, LANES), lambda b_, p, i: (b_, tmap(jnp.minimum(i, nt - 1)), p))
    blk_out = pl.BlockSpec((1, tb, LANES), lambda b_, p, i: (b_, tmap(jnp.maximum(i - 1, 0)), p))
    vec = pl.BlockSpec((1, LANES), lambda b_, p, i: (0, p))
    scratch = [
        pltpu.VMEM((RWKV_CHUNK, LANES), F32),
        pltpu.VMEM((2, nc, LANES, LANES), BF16), pltpu.VMEM((2, nc, RWKV_CHUNK, LANES), F32),
        pltpu.VMEM((2, nc, RWKV_CHUNK, LANES), BF16), pltpu.VMEM((2, nc, RWKV_CHUNK, LANES), F32),
        pltpu.VMEM((2, nc, 8, LANES), F32), pltpu.VMEM((2, tb, LANES), F32),
    ]
    if rev:
        extra_specs, extra = [blk_out] * 3 + [vec, vec], tuple(fwd)
        out_specs, out_shape = blk_out, jax.ShapeDtypeStruct((B, T, W), BF16)
        scratch.append(pltpu.VMEM((tb, LANES), F32))
    else:
        extra_specs, extra = [], ()
        out_specs, out_shape = [blk_out, blk_out], [jax.ShapeDtypeStruct((B, T, W), F32)] * 2
    return pl.pallas_call(
        functools.partial(_rwkv_kernel, rev=rev, tb=tb),
        grid=(B, RWKV_PAIRS, nt + 1),
        in_specs=[blk_in] * 6 + [vec] + extra_specs,
        out_specs=out_specs,
        out_shape=out_shape,
        scratch_shapes=scratch,
        compiler_params=_cparams(("parallel", "parallel", "arbitrary")),
        name="rwkv_bwd" if rev else "rwkv_fwd",
    )(r, lw, ke, v, kk, a, rk, *extra)


def _mla_prep_kernel(qc_ref, kv_ref, gq_ref, gkv_ref, wq_ref, wk_ref, wv_ref, ctq_ref, stq_ref, ctk_ref,
                     stk_ref, vt_ref, q_ref, k_ref, v_ref):
    def rms(x, g):
        return x * lax.rsqrt(jnp.mean(x * x, -1, keepdims=True) + EPS) * g

    qn = rms(qc_ref[0], gq_ref[...]).astype(BF16)
    kvx = kv_ref[0]
    kvn = rms(kvx[:, :MLA_KV_RANK], gkv_ref[...])
    kin = jnp.concatenate([kvn, kvx[:, MLA_KV_RANK:]], -1).astype(BF16)
    q = _dot(qn, wq_ref[...])
    k = _dot(kin, wk_ref[...])
    v_ref[0] = (_dot(kvn.astype(BF16), wv_ref[...]) + vt_ref[...]).astype(BF16)
    ctq, stq, ctk, stk = ctq_ref[...], stq_ref[...], ctk_ref[...], stk_ref[...]
    for h in range(MLA_HEADS):
        sl = slice(h * MLA_SLOT, (h + 1) * MLA_SLOT)
        qh, kh = q[:, sl], k[:, sl]
        q_ref[0, :, sl] = (qh * ctq + pltpu.roll(qh, MLA_SLOT - MLA_ROPE, 1) * stq).astype(BF16)
        k_ref[0, :, sl] = (kh * ctk + pltpu.roll(kh, MLA_SLOT - MLA_ROPE, 1) * stk).astype(BF16)


def _mla_prep(u, gq, gkv, wq, wk, wv, ctq, stq, ctk, stk, vt):
    B, T, _ = u.shape
    tm = min(T, MLA_PREP_TM)
    kvw = MLA_KV_RANK + 2 * MLA_ROPE
    QW = MLA_HEADS * MLA_SLOT
    full = lambda shape: pl.BlockSpec(shape, lambda b_, i: (0,) * len(shape))
    tab = pl.BlockSpec((tm, MLA_SLOT), lambda b_, i: (i, 0))
    tok = pl.BlockSpec((1, tm, QW), lambda b_, i: (b_, i, 0))
    return pl.pallas_call(
        _mla_prep_kernel,
        grid=(B, T // tm),
        in_specs=[
            pl.BlockSpec((1, tm, MLA_Q_RANK), lambda b_, i: (b_, i, U_QC // MLA_Q_RANK)),
            pl.BlockSpec((1, tm, kvw), lambda b_, i: (b_, i, U_KV // kvw)),
            full((1, MLA_Q_RANK)), full((1, MLA_KV_RANK)),
            full((MLA_Q_RANK, QW)), full((kvw, QW)), full((MLA_KV_RANK, QW)),
            tab, tab, tab, tab, full((1, QW)),
        ],
        out_specs=[tok, tok, tok],
        out_shape=[jax.ShapeDtypeStruct((B, T, QW), BF16)] * 3,
        compiler_params=_cparams(("parallel", "parallel")),
        name="mla_prep",
    )(u, u, gq, gkv, wq, wk, wv, ctq, stq, ctk, stk, vt)


def _flash_kernel(q_ref, k_ref, v_ref, o_ref, *, tkc, n_kv, unroll):
    q = q_ref[0]
    tq = q.shape[0]

    def step(j, carry):
        m, acc = carry
        off = pl.multiple_of(j * tkc, tkc)
        s = _dot_nt(q, k_ref[0, pl.ds(off, tkc), :])
        m_new = jnp.maximum(m, jnp.max(s, -1, keepdims=True))
        p = jnp.exp2(s - m_new)
        acc = jnp.exp2(m - m_new) * acc + _dot(p.astype(BF16), v_ref[0, pl.ds(off, tkc), :])
        return m_new, acc

    init = (jnp.full((tq, 1), -jnp.inf, F32), jnp.zeros((tq, MLA_SLOT), F32))
    _, acc = lax.fori_loop(0, n_kv, step, init, unroll=unroll)
    o_ref[0] = (acc[:, :MLA_V] / acc[:, MLA_V:]).astype(o_ref.dtype)


def _flash(q, k, v):
    B, T, _ = q.shape
    tq = min(T, ATTN_TQ)
    tkc = min(T, ATTN_TK)
    n_kv = T // tkc
    resident = lambda: pl.BlockSpec((1, T, MLA_SLOT), lambda b_, h, i: (b_, 0, h), pipeline_mode=pl.Buffered(1))
    return pl.pallas_call(
        functools.partial(_flash_kernel, tkc=tkc, n_kv=n_kv, unroll=min(n_kv, ATTN_UNROLL)),
        grid=(B, MLA_HEADS, T // tq),
        in_specs=[pl.BlockSpec((1, tq, MLA_SLOT), lambda b_, h, i: (b_, i, h)), resident(), resident()],
        out_specs=pl.BlockSpec((1, tq, MLA_V), lambda b_, h, i: (b_, i, h)),
        out_shape=jax.ShapeDtypeStruct((B, T, MLA_HEADS * MLA_V), BF16),
        compiler_params=_cparams(("parallel", "parallel", "arbitrary")),
        name="mla_attn",
    )(q, k, v)


def _mix_out_kernel(a_ref, b_ref, c_ref, gt_ref, x_ref, g1_ref, lng_ref, lnb_ref, wa_ref, wb_ref, wc_ref, wo_ref, o_ref):
    D = D_MODEL
    gt = gt_ref[0].astype(F32)
    merged = (gt[:, :D] * _dot(a_ref[0], wa_ref[...]) + gt[:, D:2 * D] * _dot(b_ref[0], wb_ref[...])
              + gt[:, 2 * D:] * _dot(c_ref[0], wc_ref[...]))
    mix = _dot(merged.astype(BF16), wo_ref[...])
    o_ref[0] = _layer_norm(ALPHA * x_ref[0] + (1.0 + g1_ref[0]) * mix, lng_ref[...], lnb_ref[...])


def _mix_out(a_in, b_in, c_in, gates, x, g1, lng, lnb, wa, wb, wc, wo):
    B, T, D = x.shape
    tm = min(T, MIX_TM)
    tok = lambda width: pl.BlockSpec((1, tm, width), lambda b_, i: (b_, i, 0))
    vec = pl.BlockSpec((1, D), lambda b_, i: (0, 0))
    mat = pl.BlockSpec((D, D), lambda b_, i: (0, 0))
    return pl.pallas_call(
        _mix_out_kernel,
        grid=(B, T // tm),
        in_specs=[tok(D), tok(D), tok(D), tok(3 * D), tok(D), pl.BlockSpec((1, 1, D), lambda b_, i: (b_, 0, 0)),
                  vec, vec, mat, mat, mat, mat],
        out_specs=tok(D),
        out_shape=jax.ShapeDtypeStruct((B, T, D), F32),
        compiler_params=_cparams(("parallel", "parallel")),
        name="mix_out",
    )(a_in, b_in, c_in, gates, x, g1, lng, lnb, wa, wb, wc, wo)


def _mlp_kernel(x_ref, sc_ref, sh_ref, g2_ref, w1_ref, w2_ref, lng_ref, lnb_ref, o_ref, h_ref, acc_ref):
    f = pl.program_id(2)

    @pl.when(f == 0)
    def _():
        h_ref[...] = (x_ref[0] * (1.0 + sc_ref[0]) + sh_ref[0]).astype(BF16)
        acc_ref[...] = jnp.zeros_like(acc_ref)

    a = jnp.maximum(_dot(h_ref[...], w1_ref[...]), 0.0)
    acc_ref[...] += _dot((a * a).astype(BF16), w2_ref[...])

    @pl.when(f == pl.num_programs(2) - 1)
    def _():
        o_ref[0] = _layer_norm(ALPHA * x_ref[0] + (1.0 + g2_ref[0]) * acc_ref[...], lng_ref[...], lnb_ref[...])


def _mlp(x, sc, sh, g2, w1, w2, lng, lnb):
    B, T, D = x.shape
    F = w1.shape[1]
    tm = min(T, PROJ_TM)
    tf = MLP_TF
    ada = pl.BlockSpec((1, 1, D), lambda b_, i, f: (b_, 0, 0))
    vec = pl.BlockSpec((1, D), lambda b_, i, f: (0, 0))
    return pl.pallas_call(
        _mlp_kernel,
        grid=(B, T // tm, F // tf),
        in_specs=[
            pl.BlockSpec((1, tm, D), lambda b_, i, f: (b_, i, 0)), ada, ada, ada,
            pl.BlockSpec((D, tf), lambda b_, i, f: (0, f)),
            pl.BlockSpec((tf, D), lambda b_, i, f: (f, 0)),
            vec, vec,
        ],
        out_specs=pl.BlockSpec((1, tm, D), lambda b_, i, f: (b_, i, 0)),
        out_shape=jax.ShapeDtypeStruct((B, T, D), F32),
        scratch_shapes=[pltpu.VMEM((tm, D), BF16), pltpu.VMEM((tm, D), F32)],
        compiler_params=_cparams(("parallel", "parallel", "arbitrary")),
        name="mlp",
    )(x, sc, sh, g2, w1, w2, lng, lnb)


def _rot_half_cols(w):
    half = w.shape[-1] // 2
    return jnp.concatenate([-w[..., half:], w[..., :half]], -1)


def _prep_layer_weights(l, w_in, rwkv_wup, rwkv_aup, mla_wuq, mla_wukv):
    D = D_MODEL
    wi = w_in[l]
    ret_rwkv = wi[:, :7168]
    lora = wi[:, 7168:7296]
    dg = wi[:, 7296:7424]
    qc = wi[:, 7424:7936]
    kvc = wi[:, 7936:8192]
    kr = wi[:, 8192:8256]
    w_u = jnp.concatenate([ret_rwkv, qc, kvc, kr, _rot_half_cols(kr), lora, dg,
                           jnp.zeros((D, U_COLS - 8320), F32)], -1).astype(BF16)

    z64 = jnp.zeros((2, 64, RWKV_W), F32)
    wl = _hi_lo(jnp.concatenate([jnp.concatenate([rwkv_wup[l], z64], 1), jnp.concatenate([z64, rwkv_aup[l]], 1)], -1))

    wq = mla_wuq[l].reshape(MLA_Q_RANK, MLA_HEADS, MLA_NOPE + MLA_ROPE)
    wq_r = wq[..., MLA_NOPE:]
    wq = jnp.concatenate([wq[..., :MLA_NOPE], wq_r, _rot_half_cols(wq_r)], -1)
    wq = wq.reshape(MLA_Q_RANK, MLA_HEADS * MLA_SLOT).astype(BF16)

    wkv = mla_wukv[l].reshape(MLA_KV_RANK, MLA_HEADS, MLA_NOPE + MLA_V)
    wk_top = jnp.concatenate([wkv[..., :MLA_NOPE], jnp.zeros((MLA_KV_RANK, MLA_HEADS, 2 * MLA_ROPE), F32)], -1)
    eye = jnp.eye(2 * MLA_ROPE, dtype=F32)[:, None, :]
    wk_bot = jnp.concatenate([jnp.zeros((2 * MLA_ROPE, MLA_HEADS, MLA_NOPE), F32),
                              jnp.broadcast_to(eye, (2 * MLA_ROPE, MLA_HEADS, 2 * MLA_ROPE))], -1)
    wk = jnp.concatenate([wk_top, wk_bot], 0).reshape(MLA_KV_RANK + 2 * MLA_ROPE, MLA_HEADS * MLA_SLOT).astype(BF16)
    wv = jnp.concatenate([wkv[..., MLA_NOPE:], jnp.zeros((MLA_KV_RANK, MLA_HEADS, MLA_SLOT - MLA_V), F32)], -1)
    wv = wv.reshape(MLA_KV_RANK, MLA_HEADS * MLA_SLOT).astype(BF16)
    return w_u, wl, wq, wk, wv


def _rope_tables(T):
    pos = jnp.arange(T, dtype=F32)

    def cs(half):
        inv = ROPE_THETA ** (-jnp.arange(half, dtype=F32) / half)
        ang = pos[:, None] * inv[None, :]
        return jnp.cos(ang), jnp.sin(ang)

    ret_cos, ret_sin = cs(RET_DK // 2)
    c, s = cs(MLA_ROPE // 2)
    ones = jnp.ones((T, MLA_NOPE), F32)
    zn = jnp.zeros((T, MLA_NOPE), F32)
    zr = jnp.zeros((T, MLA_ROPE), F32)
    ct = jnp.concatenate([ones, c, c, zr], -1)
    st = jnp.concatenate([zn, s, s, zr], -1)
    scale = (MLA_NOPE + MLA_ROPE) ** -0.5 * math.log2(math.e)
    return ret_cos, ret_sin, ct * scale, st * scale, ct, st


def _trunk(x, ada, tables, lw, consts):
    B, T, D = x.shape
    ret_cos, ret_sin, ctq, stq, ctk, stk = tables
    lg_f, lg_b, vt = consts
    tm_prep = min(T, RWKV_PREP_TM)
    for l in range(DEPTH):
        (w_u, w_mg, b_mg, wl, wq, wk, wv, p) = lw[l]
        sh1, sc1, g1, sh2, sc2, g2 = [ada[l][:, None, i * D:(i + 1) * D] for i in range(6)]
        zero_b = jnp.zeros((1, U_COLS), F32)
        u = _inproj(x, sc1, sh1, w_u, zero_b, gate=False, tn=INPROJ_TN, name="in_proj")
        gates = _inproj(x, sc1, sh1, w_mg, b_mg, gate=True, tn=GATES_TN, name="merge_gates", out_dtype=BF16)

        a_in = _retention(u, ret_cos, ret_sin, lg_b, fwd=_retention(u, ret_cos, ret_sin, lg_f), gn=p["gn"])

        outs = _rwkv_prep(u, p["mu"], p["w0"], p["a0"], wl, p["gup"], p["kkg"], p["kag"], tm=tm_prep)
        yf, bf = _rwkv_scan(*outs[0:6], p["rk"])
        b_in = _rwkv_scan(*outs[6:12], p["rk"], fwd=(yf, bf, outs[12], p["lxg"], p["lxb"]))

        q, k, v = _mla_prep(u, p["gq"], p["gkv"], wq, wk, wv, ctq, stq, ctk, stk, vt)
        attn = _flash(q, k, v)

        x = _mix_out(a_in, b_in, attn, gates, x, g1, p["ln1g"], p["ln1b"], p["wa"], p["wb"], p["wc"], p["wo"])
        x = _mlp(x, sc2, sh2, g2, p["w1"], p["w2"], p["ln2g"], p["ln2b"])
    return x


def kernel(x_prompt, x_sample, c_prompt, c_sample, w_ada, b_ada, w_in, ret_gn, ret_wo, rwkv_mu, rwkv_w0, rwkv_wup, rwkv_a0, rwkv_aup, rwkv_gup, rwkv_kk, rwkv_ka, rwkv_rk, rwkv_lnx_g, rwkv_lnx_b, rwkv_wo, mla_qnorm, mla_wuq, mla_kvnorm, mla_wukv, mla_wo, w_merge, b_merge, w_out, ln_g, ln_b, w_mlp1, w_mlp2):
    D = D_MODEL
    nb_p = c_prompt.shape[0]
    nb_s = c_sample.shape[0]
    c_pad = jnp.concatenate([c_prompt, c_sample, jnp.zeros((8 - nb_p - nb_s, D), F32)], 0)
    ada = _ada_all(c_pad, w_ada, b_ada)

    lw = []
    for l in range(DEPTH):
        w_u, wl, wq, wk, wv = _prep_layer_weights(l, w_in, rwkv_wup, rwkv_aup, mla_wuq, mla_wukv)
        p = dict(
            mu=rwkv_mu[l][:, None, :], w0=rwkv_w0[l][:, None, :], a0=rwkv_a0[l][:, None, :],
            gup=_hi_lo(rwkv_gup[l]), kkg=rwkv_kk[l][None], kag=rwkv_ka[l][None], rk=rwkv_rk[l][None],
            gq=mla_qnorm[l][None], gkv=mla_kvnorm[l][None], gn=ret_gn[l][None],
            lxg=rwkv_lnx_g[l][None], lxb=rwkv_lnx_b[l][None],
            ln1g=ln_g[l, 0][None], ln1b=ln_b[l, 0][None], ln2g=ln_g[l, 1][None], ln2b=ln_b[l, 1][None],
            wa=ret_wo[l].astype(BF16), wb=rwkv_wo[l].astype(BF16), wc=mla_wo[l].astype(BF16),
            wo=w_out[l].astype(BF16), w1=w_mlp1[l].astype(BF16), w2=w_mlp2[l].astype(BF16),
        )
        lw.append((w_u, w_merge[l].astype(BF16), b_merge[l][None], wl, wq, wk, wv, p))

    log_g = jnp.log1p(-jnp.exp2(-5.0 - jnp.arange(RET_HEADS, dtype=F32)))
    vt = jnp.tile(jnp.concatenate([jnp.zeros((1, MLA_V), F32), jnp.ones((1, MLA_SLOT - MLA_V), F32)], -1),
                  (1, MLA_HEADS))
    consts = (log_g, log_g[::-1], vt)

    y_p = _trunk(x_prompt, ada[:, :nb_p], _rope_tables(x_prompt.shape[1]), lw, consts)
    y_s = _trunk(x_sample, ada[:, nb_p:nb_p + nb_s], _rope_tables(x_sample.shape[1]), lw, consts)
    return (y_p, y_s)
```

```python
import functools
import math

import jax
import jax.numpy as jnp
from jax import lax
from jax.experimental import pallas as pl
from jax.experimental.pallas import tpu as pltpu

F32 = jnp.float32
BF16 = jnp.bfloat16
HI = lax.Precision.HIGHEST

D_MODEL = 1024
DEPTH = 4
RET_HEADS = 4
RET_DK = 256
RET_CHUNK = 128
RWKV_HEAD = 64
RWKV_HEADS = 16
RWKV_W = 1024
RWKV_PAIRS = RWKV_HEADS // 2
RWKV_CHUNK = 64
MLA_HEADS = 8
MLA_NOPE = 128
MLA_ROPE = 64
MLA_V = 128
MLA_Q_RANK = 512
MLA_KV_RANK = 256
MLA_SLOT = 256
D_FF = 4 * D_MODEL
ROPE_THETA = 10000.0
ALPHA = (2 * DEPTH) ** 0.25
EPS = 1e-5
LANES = 128

U_RET_G = 3
U_RWKV_R = 4
U_QC = 7168
U_KV = 7680
U_LORA = 8064
U_DG = 8192
U_COLS = 8448

V7X_VMEM_BYTES = 64 * 1024 * 1024
VMEM_LIMIT = V7X_VMEM_BYTES - 8 * 1024 * 1024
ADA_TN = 1536
PROJ_TM = 1024
INPROJ_TN = U_COLS // 3
GATES_TN = 1024
RET_BLOCK = 2048
RWKV_BLOCK = 1024
RWKV_STAGES = 8
RWKV_PREP_TM = 256
MLA_PREP_TM = 512
ATTN_TQ = 1024
ATTN_TK = 512
ATTN_UNROLL = 32
MIX_TM = 512
MLP_TF = 1024


def _cparams(sem):
    return pltpu.CompilerParams(dimension_semantics=sem, vmem_limit_bytes=VMEM_LIMIT)


def _dot(a, b, precision=None):
    return jnp.dot(a, b, preferred_element_type=F32, precision=precision)


def _dot_nt(a, b, precision=None):
    return lax.dot_general(a, b, (((1,), (1,)), ((), ())), preferred_element_type=F32, precision=precision)


def _dot_tn(a, b, precision=None):
    return lax.dot_general(a, b, (((0,), (0,)), ((), ())), preferred_element_type=F32, precision=precision)


def _split_bf16(x, n):
    parts = []
    for _ in range(n):
        p = x.astype(BF16)
        parts.append(p)
        x = x - p.astype(F32)
    return parts


def _hi_lo(w):
    hi = w.astype(BF16)
    return hi, (w - hi.astype(F32)).astype(BF16)


def _dot3(x, w_hi, w_lo):
    x_hi, x_lo = _split_bf16(x, 2)
    return _dot(x_hi, w_hi) + _dot(x_hi, w_lo) + _dot(x_lo, w_hi)


def _sigmoid(x):
    return 1.0 / (1.0 + jnp.exp(-x))


def _layer_norm(x, g, b):
    mu = jnp.mean(x, -1, keepdims=True)
    xc = x - mu
    var = jnp.mean(xc * xc, -1, keepdims=True)
    return xc * lax.rsqrt(var + EPS) * g + b


def _ada_kernel(c_ref, w_ref, b_ref, o_ref):
    c = c_ref[...]
    o_ref[0] = _dot(c * _sigmoid(c), w_ref[0], HI) + b_ref[0]


def _ada_all(c_pad, w_ada, b_ada):
    L, D, N = w_ada.shape
    tn = ADA_TN
    return pl.pallas_call(
        _ada_kernel,
        grid=(L, N // tn),
        in_specs=[
            pl.BlockSpec((8, D), lambda l, j: (0, 0)),
            pl.BlockSpec((1, D, tn), lambda l, j: (l, 0, j)),
            pl.BlockSpec((1, 1, tn), lambda l, j: (l, 0, j)),
        ],
        out_specs=pl.BlockSpec((1, 8, tn), lambda l, j: (l, 0, j)),
        out_shape=jax.ShapeDtypeStruct((L, 8, N), F32),
        compiler_params=_cparams(("parallel", "parallel")),
        name="ada",
    )(c_pad, w_ada, b_ada.reshape(L, 1, N))


def _inproj_kernel(x_ref, sc_ref, sh_ref, w_ref, b_ref, o_ref, h_ref, *, gate):
    @pl.when(pl.program_id(2) == 0)
    def _():
        h_ref[...] = (x_ref[0] * (1.0 + sc_ref[0]) + sh_ref[0]).astype(BF16)

    acc = _dot(h_ref[...], w_ref[...])
    if gate:
        acc = _sigmoid(acc + b_ref[...])
    o_ref[0] = acc.astype(o_ref.dtype)


def _inproj(x, sc, sh, w, b, *, gate, tn, name, out_dtype=F32):
    B, T, D = x.shape
    N = w.shape[1]
    tm = min(T, PROJ_TM)
    return pl.pallas_call(
        functools.partial(_inproj_kernel, gate=gate),
        grid=(B, T // tm, N // tn),
        in_specs=[
            pl.BlockSpec((1, tm, D), lambda b_, i, j: (b_, i, 0)),
            pl.BlockSpec((1, 1, D), lambda b_, i, j: (b_, 0, 0)),
            pl.BlockSpec((1, 1, D), lambda b_, i, j: (b_, 0, 0)),
            pl.BlockSpec((D, tn), lambda b_, i, j: (0, j)),
            pl.BlockSpec((1, tn), lambda b_, i, j: (0, j)),
        ],
        out_specs=pl.BlockSpec((1, tm, tn), lambda b_, i, j: (b_, i, j)),
        out_shape=jax.ShapeDtypeStruct((B, T, N), out_dtype),
        scratch_shapes=[pltpu.VMEM((tm, D), BF16)],
        compiler_params=_cparams(("parallel", "parallel", "arbitrary")),
        name=name,
    )(x, sc, sh, w, b)


def _ret_kernel(lg_ref, q_ref, k_ref, v_ref, cos_ref, sin_ref, *rest, rev, tb):
    if rev:
        fwd_ref, g_ref, gn_ref, o_ref, state_ref = rest
    else:
        o_ref, state_ref = rest
    C = RET_CHUNK
    half = RET_DK // 2

    @pl.when(pl.program_id(2) == 0)
    def _():
        state_ref[...] = jnp.zeros_like(state_ref)

    lg = lg_ref[pl.program_id(1)]
    row = lax.broadcasted_iota(jnp.int32, (C, C), 0).astype(F32)
    col = lax.broadcasted_iota(jnp.int32, (C, C), 1).astype(F32)
    idx = lax.broadcasted_iota(jnp.int32, (C, 1), 0).astype(F32)
    if rev:
        dist = col - row
        decay_in = jnp.where(dist > 0, jnp.exp(lg * jnp.maximum(dist, 0.0)), 0.0)
        q_scale = jnp.exp(lg * (C - idx))
        k_scale = jnp.exp(lg * idx)
    else:
        dist = row - col
        decay_in = jnp.where(dist >= 0, jnp.exp(lg * jnp.maximum(dist, 0.0)), 0.0)
        q_scale = jnp.exp(lg * (idx + 1.0))
        k_scale = jnp.exp(lg * (C - 1.0 - idx))
    chunk_decay = jnp.exp(lg * C)

    def rot(x, cos, sin):
        x1, x2 = x[:, :half], x[:, half:]
        return jnp.concatenate([x1 * cos - x2 * sin, x1 * sin + x2 * cos], -1)

    n_chunks = tb // C
    state = state_ref[...]
    for cc in range(n_chunks):
        c = n_chunks - 1 - cc if rev else cc
        rows = slice(c * C, (c + 1) * C)
        cos = cos_ref[rows, :]
        sin = sin_ref[rows, :]
        q = rot(q_ref[0, rows, :], cos, sin).astype(BF16)
        k = rot(k_ref[0, rows, :], cos, sin) * (RET_DK ** -0.5)
        v = v_ref[0, rows, :].astype(BF16)
        s = _dot_nt(q, k.astype(BF16)) * decay_in
        o = _dot(s.astype(BF16), v) + _dot(q, state.astype(BF16)) * q_scale
        state = state * chunk_decay + _dot_tn((k * k_scale).astype(BF16), v)
        if rev:
            o = o + fwd_ref[0, rows, :]
            oc = o - jnp.mean(o, -1, keepdims=True)
            on = oc * lax.rsqrt(jnp.mean(oc * oc, -1, keepdims=True) + EPS)
            g = g_ref[0, rows, :]
            o_ref[0, rows, :] = (on * gn_ref[...] * (g * _sigmoid(g))).astype(BF16)
        else:
            o_ref[0, rows, :] = o
    state_ref[...] = state


def _retention(u, cos, sin, lg, *, fwd=None, gn=None):
    rev = fwd is not None
    B, T, _ = u.shape
    tb = min(T, RET_BLOCK)
    nt = T // tb
    tmap = (lambda i: nt - 1 - i) if rev else (lambda i: i)
    head_blk = pl.BlockSpec((1, tb, RET_DK), lambda b_, h, i: (b_, tmap(i), h))
    extra_specs, extra = [], ()
    if rev:
        extra_specs = [head_blk, pl.BlockSpec((1, tb, RET_DK), lambda b_, h, i: (b_, tmap(i), 3 * RET_HEADS + h)),
                       pl.BlockSpec((1, RET_DK), lambda b_, h, i: (0, h))]
        extra = (fwd, u, gn)
    return pl.pallas_call(
        functools.partial(_ret_kernel, rev=rev, tb=tb),
        grid=(B, RET_HEADS, nt),
        in_specs=[
            pl.BlockSpec(memory_space=pltpu.SMEM),
            pl.BlockSpec((1, tb, RET_DK), lambda b_, h, i: (b_, tmap(i), h)),
            pl.BlockSpec((1, tb, RET_DK), lambda b_, h, i: (b_, tmap(i), RET_HEADS + h)),
            pl.BlockSpec((1, tb, RET_DK), lambda b_, h, i: (b_, tmap(i), 2 * RET_HEADS + h)),
            pl.BlockSpec((tb, RET_DK // 2), lambda b_, h, i: (tmap(i), 0)),
            pl.BlockSpec((tb, RET_DK // 2), lambda b_, h, i: (tmap(i), 0)),
        ] + extra_specs,
        out_specs=head_blk,
        out_shape=jax.ShapeDtypeStruct((B, T, RET_HEADS * RET_DK), BF16 if rev else F32),
        scratch_shapes=[pltpu.VMEM((RET_DK, RET_DK), F32)],
        compiler_params=_cparams(("parallel", "parallel", "arbitrary")),
        name="ret_bwd" if rev else "ret_fwd",
    )(lg, u, u, u, cos, sin, *extra)


def _rwkv_prep_kernel(r_ref, k_ref, v_ref, l_ref, dg_ref, pr_ref, pk_ref, pv_ref, pl_ref, nr_ref, nk_ref, nv_ref,
                      nl_ref, mu_ref, w0_ref, a0_ref, wl_hi_ref, wl_lo_ref, gup_hi_ref, gup_lo_ref, kkg_ref, kag_ref,
                      *out_refs, tm):
    g_ref = out_refs[-1]
    W = RWKV_W
    xs = (r_ref[0], k_ref[0], v_ref[0], l_ref[0])
    offs = (0, W, 2 * W, 3 * W)
    row = lax.broadcasted_iota(jnp.int32, (tm, 1), 0)
    lane = lax.broadcasted_iota(jnp.int32, (tm, LANES), 1)
    i = pl.program_id(1)
    has_prev = (i > 0).astype(F32)
    has_next = (i < pl.num_programs(1) - 1).astype(F32)
    prevs = [ref[0, 7:8, :] * has_prev for ref in (pr_ref, pk_ref, pv_ref, pl_ref)]
    nexts = [ref[0, 0:1, :] * has_next for ref in (nr_ref, nk_ref, nv_ref, nl_ref)]
    for d in range(2):
        nbs, edge, shift = (prevs, 0, 1) if d == 0 else (nexts, tm - 1, tm - 1)
        mu = mu_ref[d]
        xd = []
        for x, off, nb in zip(xs, offs, nbs):
            w = x.shape[1]
            shifted = jnp.where(row == edge, nb, pltpu.roll(x, shift, 0))
            xd.append(x + (shifted - x) * mu[:, off:off + w])
        xr, xk, xv, xl = xd
        lhs = jnp.where(lane < 64, jnp.tanh(xl), xl)
        logits = _dot3(lhs, wl_hi_ref[d], wl_lo_ref[d])
        lw = -math.exp(-0.5) * _sigmoid(w0_ref[d] + logits[:, :W])
        a = _sigmoid(a0_ref[d] + logits[:, W:])
        o = out_refs[6 * d:6 * d + 6]
        o[0][0] = xr
        o[1][0] = lw
        o[2][0] = xk * (1.0 + (a - 1.0) * kag_ref[...])
        o[3][0] = xv
        o[4][0] = xk * kkg_ref[...]
        o[5][0] = a
    g_ref[0] = _dot3(_sigmoid(dg_ref[0]), gup_hi_ref[...], gup_lo_ref[...])


def _rwkv_prep(u, mu, w0, a0, wl, gup, kkg, kag, *, tm):
    B, T, _ = u.shape
    W = RWKV_W
    full = lambda shape: pl.BlockSpec(shape, lambda b_, i: (0,) * len(shape))
    ublk = lambda width, cb: pl.BlockSpec((1, tm, width), lambda b_, i: (b_, i, cb))
    r8 = tm // 8
    before = lambda width, cb: pl.BlockSpec((1, 8, width), lambda b_, i: (b_, jnp.maximum(i * r8 - 1, 0), cb))
    after = lambda width, cb: pl.BlockSpec((1, 8, width), lambda b_, i: (b_, jnp.minimum((i + 1) * r8, T // 8 - 1), cb))
    shifted_cols = [(W, U_RWKV_R), (W, U_RWKV_R + 1), (W, U_RWKV_R + 2), (LANES, U_LORA // LANES)]
    out = jax.ShapeDtypeStruct((B, T, W), F32)
    return pl.pallas_call(
        functools.partial(_rwkv_prep_kernel, tm=tm),
        grid=(B, T // tm),
        in_specs=[
            ublk(W, U_RWKV_R), ublk(W, U_RWKV_R + 1), ublk(W, U_RWKV_R + 2),
            ublk(LANES, U_LORA // LANES), ublk(LANES, U_DG // LANES),
            *[before(w, cb) for w, cb in shifted_cols], *[after(w, cb) for w, cb in shifted_cols],
            full((2, 1, 3 * W + LANES)), full((2, 1, W)), full((2, 1, W)),
            full((2, LANES, 2 * W)), full((2, LANES, 2 * W)), full((LANES, W)), full((LANES, W)),
            full((1, W)), full((1, W)),
        ],
        out_specs=[pl.BlockSpec((1, tm, W), lambda b_, i: (b_, i, 0))] * 13,
        out_shape=[out] * 13,
        compiler_params=_cparams(("parallel", "parallel")),
        name="rwkv_prep",
    )(*([u] * 13), mu, w0, a0, *wl, *gup, kkg, kag)


def _rwkv_kernel(r_ref, lw_ref, ke_ref, v_ref, kk_ref, a_ref, rk_ref, *rest, rev, tb, nt):
    if rev:
        (yf_ref, bf_ref, wg_ref, lxg_ref, lxb_ref, o_ref,
         state_ref, p_ref, q_ref, ry_ref, y1_ref, et_ref, bs_ref, ys_ref) = rest
    else:
        y_ref, bon_ref, state_ref, p_ref, q_ref, ry_ref, y1_ref, et_ref, bs_ref = rest
    C = RWKV_CHUNK
    N = 2 * C

    @pl.when(pl.program_id(2) == 0)
    def _():
        state_ref[...] = jnp.zeros_like(state_ref)
        for ref in (p_ref, q_ref, ry_ref, y1_ref, et_ref, bs_ref):
            ref[1] = jnp.zeros(ref.shape[1:], ref.dtype)

    rowi = lax.broadcasted_iota(jnp.int32, (N, N), 0)
    coli = lax.broadcasted_iota(jnp.int32, (N, N), 1)
    head_ones = ((rowi // C) == (coli // C)).astype(BF16)
    tr = lax.broadcasted_iota(jnp.int32, (C, N), 0)
    tc = lax.broadcasted_iota(jnp.int32, (C, N), 1) % C
    before = (tc > tr) if rev else (tc < tr)
    strict = before.astype(F32)
    incl = (before | (tc == tr)).astype(F32)
    eye = (tc == tr).astype(F32)
    level_masks = []
    b = 1
    while b < C:
        level_masks.append(((tr // (2 * b) == tc // (2 * b)) & (tr // b != tc // b)).astype(F32))
        b *= 2
    ci = lax.broadcasted_iota(jnp.int32, (C, C), 0)
    cj = lax.broadcasted_iota(jnp.int32, (C, C), 1)
    cum_mat = ((cj >= ci) if rev else (cj <= ci)).astype(BF16)
    rk = rk_ref[...]

    def stack(x):
        return jnp.concatenate([x, x], axis=0) * head_ones

    def head_sum(x):
        return sum(_dot(p, head_ones) for p in _split_bf16(x, 2))

    nc = tb // C
    order = [nc - 1 - j if rev else j for j in range(nc)]
    every = lambda f, *lists: [f(*args) for args in zip(*lists)]

    i = pl.program_id(2)
    slot = i % 2
    prev = 1 - slot

    def recurrence():
        carry = [state_ref[...]]

        def recur(stage):
            for j in range(stage * nc // RWKV_STAGES, (stage + 1) * nc // RWKV_STAGES):
                c = order[j]
                s = carry[0]
                sb = s.astype(BF16)
                y = _dot_nt(ry_ref[prev, c], stack(sb)) + y1_ref[prev, c]
                if rev:
                    ys_ref[c * C:(c + 1) * C, :] = y
                else:
                    y_ref[0, c * C:(c + 1) * C, :] = y
                carry[0] = s * et_ref[prev, c][0:1] + _dot(sb, p_ref[prev, c]) + q_ref[prev, c]

        def finish():
            state_ref[...] = carry[0]
            if rev:
                y = ys_ref[...] + yf_ref[0]
                yc = y - head_sum(y) * (1.0 / RWKV_HEAD)
                var = head_sum(yc * yc) * (1.0 / RWKV_HEAD)
                o = yc * lax.rsqrt(var + EPS) * lxg_ref[...] + lxb_ref[...] + bs_ref[prev] + bf_ref[0]
                o_ref[0] = (o * wg_ref[0]).astype(BF16)
            else:
                bon_ref[0] = bs_ref[prev]

        return recur, finish

    @pl.when(i == nt)
    def _():
        recur, finish = recurrence()
        for stage in range(RWKV_STAGES):
            recur(stage)
        finish()

    @pl.when(i < nt)
    def _():
        recur, finish = recurrence()
        kk_all = kk_ref[0]
        r_all, ke_all, v_all = r_ref[0], ke_ref[0], v_ref[0]
        kk_all = kk_all * lax.rsqrt(head_sum(kk_all * kk_all) + 1e-12)
        bs_ref[slot] = head_sum(r_all * ke_all * rk) * v_all
        beta_all = kk_all * a_ref[0]
        lw_all = lw_ref[0]
        rows = [slice(c * C, (c + 1) * C) for c in order]
        lw_parts = _split_bf16(lw_all, 3)
        cum = [sum(_dot(cum_mat, p[rw]) for p in lw_parts) for rw in rows]
        recur(0)
        tot = [cm[0:1] if rev else cm[C - 1:C] for cm in cum]
        e_neg = every(lambda cm: jnp.exp(-cm), cum)
        e_end = every(lambda cm, t: jnp.exp(t - cm), cum, tot)
        a_p = every(lambda rw, cm: (-kk_all[rw] * jnp.exp(cm - lw_all[rw])).astype(BF16), rows, cum)
        r_pf = every(lambda rw, cm: r_all[rw] * jnp.exp(cm), rows, cum)
        r_p = every(lambda x: x.astype(BF16), r_pf)
        a_st = every(stack, a_p)
        b_st = every(lambda rw, e: stack((beta_all[rw] * e).astype(BF16)), rows, e_neg)
        k_st = every(lambda rw, e: stack((ke_all[rw] * e).astype(BF16)), rows, e_neg)
        v_st = every(lambda rw: stack(v_all[rw].astype(BF16)), rows)
        bh_st = every(lambda rw, e: stack((beta_all[rw] * e).astype(BF16)), rows, e_end)
        kh_st = every(lambda rw, e: stack((ke_all[rw] * e).astype(BF16)), rows, e_end)

        gram = every(lambda a_, r_, b_, k_: _dot_nt(jnp.concatenate([a_, r_], 0), jnp.concatenate([b_, k_], 0)),
                     a_p, r_p, b_st, k_st)
        recur(1)
        m_ab = every(lambda g: g[:C, :N] * strict, gram)
        m_ak = every(lambda g: (g[:C, N:] * strict).astype(BF16), gram)
        n_rb = every(lambda g: (g[C:, :N] * incl).astype(BF16), gram)
        n_rk = every(lambda g: (g[C:, N:] * incl).astype(BF16), gram)

        mab_st = every(lambda m: stack(m.astype(BF16)), m_ab)
        x = every(lambda m: eye + m * level_masks[0], m_ab)
        for lvl, mask in enumerate(level_masks[1:]):
            xb = every(lambda x_: x_.astype(BF16), x)
            t = every(lambda xb_, m: _dot(xb_, m).astype(BF16), xb, mab_st)
            x = every(lambda x_, t_, xb_: x_ + _dot(t_, stack(xb_)) * mask, x, t, xb)
            recur(2 + lvl)
        xb = every(lambda x_: x_.astype(BF16), x)

        z_st = every(lambda m, v_: stack(_dot(m, v_).astype(BF16)), m_ak, v_st)
        w12 = every(lambda xb_, a_, z_: _dot(xb_, jnp.concatenate([a_, z_], 1)).astype(BF16), xb, a_st, z_st)
        w12 = every(lambda w: jnp.concatenate([stack(w[:, :N]), stack(w[:, N:])], 1), w12)
        recur(RWKV_STAGES - 1)
        finish()
        nw = every(_dot, n_rb, w12)
        y0 = every(_dot, n_rk, v_st)
        bw = every(_dot_tn, w12, bh_st)
        kv = every(_dot_tn, v_st, kh_st)
        for j, c in enumerate(order):
            ry_ref[slot, c] = (r_pf[j] + nw[j][:, :N]).astype(BF16)
            y1_ref[slot, c] = nw[j][:, N:] + y0[j]
            p_ref[slot, c] = bw[j][:N].astype(BF16)
            q_st = bw[j][N:] + kv[j]
            q_ref[slot, c] = q_st[:C] + q_st[C:]
            et_ref[slot, c] = jnp.broadcast_to(jnp.exp(tot[j]), (8, N))


def _rwkv_scan(r, lw, ke, v, kk, a, rk, *, fwd=None):
    rev = fwd is not None
    B, T, W = r.shape
    tb = min(T, RWKV_BLOCK)
    nt = T // tb
    nc = tb // RWKV_CHUNK
    assert nc % RWKV_STAGES == 0, "the recurrence steps are spread evenly over the preparation stages"
    tmap = (lambda i: nt - 1 - i) if rev else (lambda i: i)
    blk_in = pl.BlockSpec((1, tb, LANES), lambda b_, p, i: (b_, tmap(jnp.minimum(i, nt - 1)), p))
    blk_out = pl.BlockSpec((1, tb, LANES), lambda b_, p, i: (b_, tmap(jnp.maximum(i - 1, 0)), p))
    vec = pl.BlockSpec((1, LANES), lambda b_, p, i: (0, p))
    scratch = [
        pltpu.VMEM((RWKV_CHUNK, LANES), F32),
        pltpu.VMEM((2, nc, LANES, LANES), BF16), pltpu.VMEM((2, nc, RWKV_CHUNK, LANES), F32),
        pltpu.VMEM((2, nc, RWKV_CHUNK, LANES), BF16), pltpu.VMEM((2, nc, RWKV_CHUNK, LANES), F32),
        pltpu.VMEM((2, nc, 8, LANES), F32), pltpu.VMEM((2, tb, LANES), F32),
    ]
    if rev:
        extra_specs, extra = [blk_out] * 3 + [vec, vec], tuple(fwd)
        out_specs, out_shape = blk_out, jax.ShapeDtypeStruct((B, T, W), BF16)
        scratch.append(pltpu.VMEM((tb, LANES), F32))
    else:
        extra_specs, extra = [], ()
        out_specs, out_shape = [blk_out, blk_out], [jax.ShapeDtypeStruct((B, T, W), F32)] * 2
    return pl.pallas_call(
        functools.partial(_rwkv_kernel, rev=rev, tb=tb, nt=nt),
        grid=(B, RWKV_PAIRS, nt + 1),
        in_specs=[blk_in] * 6 + [vec] + extra_specs,
        out_specs=out_specs,
        out_shape=out_shape,
        scratch_shapes=scratch,
        compiler_params=_cparams(("parallel", "parallel", "arbitrary")),
        name="rwkv_bwd" if rev else "rwkv_fwd",
    )(r, lw, ke, v, kk, a, rk, *extra)


def _mla_prep_kernel(qc_ref, kv_ref, gq_ref, gkv_ref, wq_ref, wk_ref, wv_ref, ctq_ref, stq_ref, ctk_ref,
                     stk_ref, vt_ref, q_ref, k_ref, v_ref):
    def rms(x, g):
        return x * lax.rsqrt(jnp.mean(x * x, -1, keepdims=True) + EPS) * g

    qn = rms(qc_ref[0], gq_ref[...]).astype(BF16)
    kvx = kv_ref[0]
    kvn = rms(kvx[:, :MLA_KV_RANK], gkv_ref[...])
    kin = jnp.concatenate([kvn, kvx[:, MLA_KV_RANK:]], -1).astype(BF16)
    q = _dot(qn, wq_ref[...])
    k = _dot(kin, wk_ref[...])
    v_ref[0] = (_dot(kvn.astype(BF16), wv_ref[...]) + vt_ref[...]).astype(BF16)
    ctq, stq, ctk, stk = ctq_ref[...], stq_ref[...], ctk_ref[...], stk_ref[...]
    for h in range(MLA_HEADS):
        sl = slice(h * MLA_SLOT, (h + 1) * MLA_SLOT)
        qh, kh = q[:, sl], k[:, sl]
        q_ref[0, :, sl] = (qh * ctq + pltpu.roll(qh, MLA_SLOT - MLA_ROPE, 1) * stq).astype(BF16)
        k_ref[0, :, sl] = (kh * ctk + pltpu.roll(kh, MLA_SLOT - MLA_ROPE, 1) * stk).astype(BF16)


def _mla_prep(u, gq, gkv, wq, wk, wv, ctq, stq, ctk, stk, vt):
    B, T, _ = u.shape
    tm = min(T, MLA_PREP_TM)
    kvw = MLA_KV_RANK + 2 * MLA_ROPE
    QW = MLA_HEADS * MLA_SLOT
    full = lambda shape: pl.BlockSpec(shape, lambda b_, i: (0,) * len(shape))
    tab = pl.BlockSpec((tm, MLA_SLOT), lambda b_, i: (i, 0))
    tok = pl.BlockSpec((1, tm, QW), lambda b_, i: (b_, i, 0))
    return pl.pallas_call(
        _mla_prep_kernel,
        grid=(B, T // tm),
        in_specs=[
            pl.BlockSpec((1, tm, MLA_Q_RANK), lambda b_, i: (b_, i, U_QC // MLA_Q_RANK)),
            pl.BlockSpec((1, tm, kvw), lambda b_, i: (b_, i, U_KV // kvw)),
            full((1, MLA_Q_RANK)), full((1, MLA_KV_RANK)),
            full((MLA_Q_RANK, QW)), full((kvw, QW)), full((MLA_KV_RANK, QW)),
            tab, tab, tab, tab, full((1, QW)),
        ],
        out_specs=[tok, tok, tok],
        out_shape=[jax.ShapeDtypeStruct((B, T, QW), BF16)] * 3,
        compiler_params=_cparams(("parallel", "parallel")),
        name="mla_prep",
    )(u, u, gq, gkv, wq, wk, wv, ctq, stq, ctk, stk, vt)


def _flash_kernel(q_ref, k_ref, v_ref, o_ref, *, tkc, n_kv, unroll):
    q = q_ref[0]
    tq = q.shape[0]

    def step(j, carry):
        m, acc = carry
        off = pl.multiple_of(j * tkc, tkc)
        s = _dot_nt(q, k_ref[0, pl.ds(off, tkc), :])
        m_new = jnp.maximum(m, jnp.max(s, -1, keepdims=True))
        p = jnp.exp2(s - m_new)
        acc = jnp.exp2(m - m_new) * acc + _dot(p.astype(BF16), v_ref[0, pl.ds(off, tkc), :])
        return m_new, acc

    init = (jnp.full((tq, 1), -jnp.inf, F32), jnp.zeros((tq, MLA_SLOT), F32))
    _, acc = lax.fori_loop(0, n_kv, step, init, unroll=unroll)
    o_ref[0] = (acc[:, :MLA_V] / acc[:, MLA_V:]).astype(o_ref.dtype)


def _flash(q, k, v):
    B, T, _ = q.shape
    tq = min(T, ATTN_TQ)
    tkc = min(T, ATTN_TK)
    n_kv = T // tkc
    resident = lambda: pl.BlockSpec((1, T, MLA_SLOT), lambda b_, h, i: (b_, 0, h), pipeline_mode=pl.Buffered(1))
    return pl.pallas_call(
        functools.partial(_flash_kernel, tkc=tkc, n_kv=n_kv, unroll=min(n_kv, ATTN_UNROLL)),
        grid=(B, MLA_HEADS, T // tq),
        in_specs=[pl.BlockSpec((1, tq, MLA_SLOT), lambda b_, h, i: (b_, i, h)), resident(), resident()],
        out_specs=pl.BlockSpec((1, tq, MLA_V), lambda b_, h, i: (b_, i, h)),
        out_shape=jax.ShapeDtypeStruct((B, T, MLA_HEADS * MLA_V), BF16),
        compiler_params=_cparams(("parallel", "parallel", "arbitrary")),
        name="mla_attn",
    )(q, k, v)


def _mix_out_kernel(a_ref, b_ref, c_ref, gt_ref, x_ref, g1_ref, lng_ref, lnb_ref, wa_ref, wb_ref, wc_ref, wo_ref, o_ref):
    D = D_MODEL
    gt = gt_ref[0].astype(F32)
    merged = (gt[:, :D] * _dot(a_ref[0], wa_ref[...]) + gt[:, D:2 * D] * _dot(b_ref[0], wb_ref[...])
              + gt[:, 2 * D:] * _dot(c_ref[0], wc_ref[...]))
    mix = _dot(merged.astype(BF16), wo_ref[...])
    o_ref[0] = _layer_norm(ALPHA * x_ref[0] + (1.0 + g1_ref[0]) * mix, lng_ref[...], lnb_ref[...])


def _mix_out(a_in, b_in, c_in, gates, x, g1, lng, lnb, wa, wb, wc, wo):
    B, T, D = x.shape
    tm = min(T, MIX_TM)
    tok = lambda width: pl.BlockSpec((1, tm, width), lambda b_, i: (b_, i, 0))
    vec = pl.BlockSpec((1, D), lambda b_, i: (0, 0))
    mat = pl.BlockSpec((D, D), lambda b_, i: (0, 0))
    return pl.pallas_call(
        _mix_out_kernel,
        grid=(B, T // tm),
        in_specs=[tok(D), tok(D), tok(D), tok(3 * D), tok(D), pl.BlockSpec((1, 1, D), lambda b_, i: (b_, 0, 0)),
                  vec, vec, mat, mat, mat, mat],
        out_specs=tok(D),
        out_shape=jax.ShapeDtypeStruct((B, T, D), F32),
        compiler_params=_cparams(("parallel", "parallel")),
        name="mix_out",
    )(a_in, b_in, c_in, gates, x, g1, lng, lnb, wa, wb, wc, wo)


def _mlp_kernel(x_ref, sc_ref, sh_ref, g2_ref, w1_ref, w2_ref, lng_ref, lnb_ref, o_ref, h_ref, acc_ref):
    f = pl.program_id(2)

    @pl.when(f == 0)
    def _():
        h_ref[...] = (x_ref[0] * (1.0 + sc_ref[0]) + sh_ref[0]).astype(BF16)
        acc_ref[...] = jnp.zeros_like(acc_ref)

    a = jnp.maximum(_dot(h_ref[...], w1_ref[...]), 0.0)
    acc_ref[...] += _dot((a * a).astype(BF16), w2_ref[...])

    @pl.when(f == pl.num_programs(2) - 1)
    def _():
        o_ref[0] = _layer_norm(ALPHA * x_ref[0] + (1.0 + g2_ref[0]) * acc_ref[...], lng_ref[...], lnb_ref[...])


def _mlp(x, sc, sh, g2, w1, w2, lng, lnb):
    B, T, D = x.shape
    F = w1.shape[1]
    tm = min(T, PROJ_TM)
    tf = MLP_TF
    ada = pl.BlockSpec((1, 1, D), lambda b_, i, f: (b_, 0, 0))
    vec = pl.BlockSpec((1, D), lambda b_, i, f: (0, 0))
    return pl.pallas_call(
        _mlp_kernel,
        grid=(B, T // tm, F // tf),
        in_specs=[
            pl.BlockSpec((1, tm, D), lambda b_, i, f: (b_, i, 0)), ada, ada, ada,
            pl.BlockSpec((D, tf), lambda b_, i, f: (0, f)),
            pl.BlockSpec((tf, D), lambda b_, i, f: (f, 0)),
            vec, vec,
        ],
        out_specs=pl.BlockSpec((1, tm, D), lambda b_, i, f: (b_, i, 0)),
        out_shape=jax.ShapeDtypeStruct((B, T, D), F32),
        scratch_shapes=[pltpu.VMEM((tm, D), BF16), pltpu.VMEM((tm, D), F32)],
        compiler_params=_cparams(("parallel", "parallel", "arbitrary")),
        name="mlp",
    )(x, sc, sh, g2, w1, w2, lng, lnb)


def _rot_half_cols(w):
    half = w.shape[-1] // 2
    return jnp.concatenate([-w[..., half:], w[..., :half]], -1)


def _prep_layer_weights(l, w_in, rwkv_wup, rwkv_aup, mla_wuq, mla_wukv):
    D = D_MODEL
    wi = w_in[l]
    ret_rwkv = wi[:, :7168]
    lora = wi[:, 7168:7296]
    dg = wi[:, 7296:7424]
    qc = wi[:, 7424:7936]
    kvc = wi[:, 7936:8192]
    kr = wi[:, 8192:8256]
    w_u = jnp.concatenate([ret_rwkv, qc, kvc, kr, _rot_half_cols(kr), lora, dg,
                           jnp.zeros((D, U_COLS - 8320), F32)], -1).astype(BF16)

    z64 = jnp.zeros((2, 64, RWKV_W), F32)
    wl = _hi_lo(jnp.concatenate([jnp.concatenate([rwkv_wup[l], z64], 1), jnp.concatenate([z64, rwkv_aup[l]], 1)], -1))

    wq = mla_wuq[l].reshape(MLA_Q_RANK, MLA_HEADS, MLA_NOPE + MLA_ROPE)
    wq_r = wq[..., MLA_NOPE:]
    wq = jnp.concatenate([wq[..., :MLA_NOPE], wq_r, _rot_half_cols(wq_r)], -1)
    wq = wq.reshape(MLA_Q_RANK, MLA_HEADS * MLA_SLOT).astype(BF16)

    wkv = mla_wukv[l].reshape(MLA_KV_RANK, MLA_HEADS, MLA_NOPE + MLA_V)
    wk_top = jnp.concatenate([wkv[..., :MLA_NOPE], jnp.zeros((MLA_KV_RANK, MLA_HEADS, 2 * MLA_ROPE), F32)], -1)
    eye = jnp.eye(2 * MLA_ROPE, dtype=F32)[:, None, :]
    wk_bot = jnp.concatenate([jnp.zeros((2 * MLA_ROPE, MLA_HEADS, MLA_NOPE), F32),
                              jnp.broadcast_to(eye, (2 * MLA_ROPE, MLA_HEADS, 2 * MLA_ROPE))], -1)
    wk = jnp.concatenate([wk_top, wk_bot], 0).reshape(MLA_KV_RANK + 2 * MLA_ROPE, MLA_HEADS * MLA_SLOT).astype(BF16)
    wv = jnp.concatenate([wkv[..., MLA_NOPE:], jnp.zeros((MLA_KV_RANK, MLA_HEADS, MLA_SLOT - MLA_V), F32)], -1)
    wv = wv.reshape(MLA_KV_RANK, MLA_HEADS * MLA_SLOT).astype(BF16)
    return w_u, wl, wq, wk, wv


def _rope_tables(T):
    pos = jnp.arange(T, dtype=F32)

    def cs(half):
        inv = ROPE_THETA ** (-jnp.arange(half, dtype=F32) / half)
        ang = pos[:, None] * inv[None, :]
        return jnp.cos(ang), jnp.sin(ang)

    ret_cos, ret_sin = cs(RET_DK // 2)
    c, s = cs(MLA_ROPE // 2)
    ones = jnp.ones((T, MLA_NOPE), F32)
    zn = jnp.zeros((T, MLA_NOPE), F32)
    zr = jnp.zeros((T, MLA_ROPE), F32)
    ct = jnp.concatenate([ones, c, c, zr], -1)
    st = jnp.concatenate([zn, s, s, zr], -1)
    scale = (MLA_NOPE + MLA_ROPE) ** -0.5 * math.log2(math.e)
    return ret_cos, ret_sin, ct * scale, st * scale, ct, st


def _trunk(x, ada, tables, lw, consts):
    B, T, D = x.shape
    ret_cos, ret_sin, ctq, stq, ctk, stk = tables
    lg_f, lg_b, vt = consts
    tm_prep = min(T, RWKV_PREP_TM)
    for l in range(DEPTH):
        (w_u, w_mg, b_mg, wl, wq, wk, wv, p) = lw[l]
        sh1, sc1, g1, sh2, sc2, g2 = [ada[l][:, None, i * D:(i + 1) * D] for i in range(6)]
        zero_b = jnp.zeros((1, U_COLS), F32)
        u = _inproj(x, sc1, sh1, w_u, zero_b, gate=False, tn=INPROJ_TN, name="in_proj")
        gates = _inproj(x, sc1, sh1, w_mg, b_mg, gate=True, tn=GATES_TN, name="merge_gates", out_dtype=BF16)

        a_in = _retention(u, ret_cos, ret_sin, lg_b, fwd=_retention(u, ret_cos, ret_sin, lg_f), gn=p["gn"])

        outs = _rwkv_prep(u, p["mu"], p["w0"], p["a0"], wl, p["gup"], p["kkg"], p["kag"], tm=tm_prep)
        yf, bf = _rwkv_scan(*outs[0:6], p["rk"])
        b_in = _rwkv_scan(*outs[6:12], p["rk"], fwd=(yf, bf, outs[12], p["lxg"], p["lxb"]))

        q, k, v = _mla_prep(u, p["gq"], p["gkv"], wq, wk, wv, ctq, stq, ctk, stk, vt)
        attn = _flash(q, k, v)

        x = _mix_out(a_in, b_in, attn, gates, x, g1, p["ln1g"], p["ln1b"], p["wa"], p["wb"], p["wc"], p["wo"])
        x = _mlp(x, sc2, sh2, g2, p["w1"], p["w2"], p["ln2g"], p["ln2b"])
    return x


def kernel(x_prompt, x_sample, c_prompt, c_sample, w_ada, b_ada, w_in, ret_gn, ret_wo, rwkv_mu, rwkv_w0, rwkv_wup, rwkv_a0, rwkv_aup, rwkv_gup, rwkv_kk, rwkv_ka, rwkv_rk, rwkv_lnx_g, rwkv_lnx_b, rwkv_wo, mla_qnorm, mla_wuq, mla_kvnorm, mla_wukv, mla_wo, w_merge, b_merge, w_out, ln_g, ln_b, w_mlp1, w_mlp2):
    D = D_MODEL
    nb_p = c_prompt.shape[0]
    nb_s = c_sample.shape[0]
    c_pad = jnp.concatenate([c_prompt, c_sample, jnp.zeros((8 - nb_p - nb_s, D), F32)], 0)
    ada = _ada_all(c_pad, w_ada, b_ada)

    lw = []
    for l in range(DEPTH):
        w_u, wl, wq, wk, wv = _prep_layer_weights(l, w_in, rwkv_wup, rwkv_aup, mla_wuq, mla_wukv)
        p = dict(
            mu=rwkv_mu[l][:, None, :], w0=rwkv_w0[l][:, None, :], a0=rwkv_a0[l][:, None, :],
            gup=_hi_lo(rwkv_gup[l]), kkg=rwkv_kk[l][None], kag=rwkv_ka[l][None], rk=rwkv_rk[l][None],
            gq=mla_qnorm[l][None], gkv=mla_kvnorm[l][None], gn=ret_gn[l][None],
            lxg=rwkv_lnx_g[l][None], lxb=rwkv_lnx_b[l][None],
            ln1g=ln_g[l, 0][None], ln1b=ln_b[l, 0][None], ln2g=ln_g[l, 1][None], ln2b=ln_b[l, 1][None],
            wa=ret_wo[l].astype(BF16), wb=rwkv_wo[l].astype(BF16), wc=mla_wo[l].astype(BF16),
            wo=w_out[l].astype(BF16), w1=w_mlp1[l].astype(BF16), w2=w_mlp2[l].astype(BF16),
        )
        lw.append((w_u, w_merge[l].astype(BF16), b_merge[l][None], wl, wq, wk, wv, p))

    log_g = jnp.log1p(-jnp.exp2(-5.0 - jnp.arange(RET_HEADS, dtype=F32)))
    vt = jnp.tile(jnp.concatenate([jnp.zeros((1, MLA_V), F32), jnp.ones((1, MLA_SLOT - MLA_V), F32)], -1),
                  (1, MLA_HEADS))
    consts = (log_g, log_g[::-1], vt)

    y_p = _trunk(x_prompt, ada[:, :nb_p], _rope_tables(x_prompt.shape[1]), lw, consts)
    y_s = _trunk(x_sample, ada[:, nb_p:nb_p + nb_s], _rope_tables(x_sample.shape[1]), lw, consts)
    return (y_p, y_s)
```
